```python
import math
import jax, jax.numpy as jnp
from jax import lax
import numpy as np

D_MODEL = 1024
BATCH = 16
SEQ = 2048
DEPTH = 4

N_META = 16
CHUNK = 64
N_PAD = (-N_META) % CHUNK
DN_HEADS = 4
DN_HEAD_DIM = 128
DN_WIDTH = DN_HEADS * DN_HEAD_DIM
SC_GROUPS = 4
SC_WIDTH = D_MODEL - DN_WIDTH
MIX_WIDTH = DN_WIDTH + SC_WIDTH
DN_CONV = 3
SC_CONV = 3
D_FF = ((-(-8 * D_MODEL // 3) + 255) // 256) * 256
RMS_EPS = 1e-6
L2_EPS = 1e-6
IN_SIZES = (3 * DN_WIDTH, DN_WIDTH, 2 * DN_HEADS, 2 * DN_HEADS, SC_WIDTH, SC_WIDTH, SC_WIDTH)
IN_WIDTH = 4 * DN_WIDTH + 4 * DN_HEADS + 3 * SC_WIDTH

kernel_name = "hymba_gdn_shortconv_encoder"


def rmsnorm(x, w):
    xf = x.astype(jnp.float32)
    y = xf * lax.rsqrt(jnp.mean(xf * xf, axis=-1, keepdims=True) + RMS_EPS)
    return (y * w.astype(jnp.float32)).astype(x.dtype)


def l2norm(x):
    xf = x.astype(jnp.float32)
    return xf * lax.rsqrt(jnp.sum(xf * xf, axis=-1, keepdims=True) + L2_EPS)


def depthwise_conv_centred(x, w):
    K, C = w.shape
    return lax.conv_general_dilated(
        x, w.astype(x.dtype)[:, None, :], window_strides=(1,),
        padding=[(K // 2, K // 2)], dimension_numbers=('NWC', 'WIO', 'NWC'),
        feature_group_count=C)


def chunk_gated_delta_rule(q, k, v, g, beta):
    Bsz, T, H, Dk = q.shape
    Dv = v.shape[-1]
    N = T // CHUNK

    def to_chunks(t):
        return jnp.swapaxes(t.reshape((Bsz, N, CHUNK) + t.shape[2:]), 2, 3)

    q = to_chunks(q) * (Dk ** -0.5)
    k = to_chunks(k)
    v = to_chunks(v)
    g = to_chunks(g)
    beta = to_chunks(beta)

    G = jnp.cumsum(g, axis=-1)
    incl = jnp.tril(jnp.ones((CHUNK, CHUNK), dtype=bool))
    strict = jnp.tril(jnp.ones((CHUNK, CHUNK), dtype=bool), -1)
    decay_mat = jnp.exp(jnp.where(incl, G[..., :, None] - G[..., None, :], -jnp.inf))

    kb = k * beta[..., None]
    vb = v * beta[..., None]
    M = jnp.where(strict, jnp.einsum('bnhid,bnhjd->bnhij', kb, k) * decay_mat, 0.0)
    A = M + jnp.eye(CHUNK, dtype=M.dtype)
    u = lax.linalg.triangular_solve(A, vb, left_side=True, lower=True, unit_diagonal=True)
    w = lax.linalg.triangular_solve(A, kb * jnp.exp(G)[..., None], left_side=True, lower=True,
                                    unit_diagonal=True)
    attn = jnp.einsum('bnhid,bnhjd->bnhij', q, k) * decay_mat
    q_dec = q * jnp.exp(G)[..., None]
    k_dec = k * jnp.exp(G[..., -1:] - G)[..., None]
    g_last = jnp.exp(G[..., -1])

    def step(S, inp):
        u_c, w_c, qd_c, kd_c, a_c, gl_c = inp
        v_new = u_c - jnp.einsum('bhcd,bhde->bhce', w_c, S)
        o = jnp.einsum('bhcd,bhde->bhce', qd_c, S) + jnp.einsum('bhij,bhje->bhie', a_c, v_new)
        S = S * gl_c[..., None, None] + jnp.einsum('bhcd,bhce->bhde', kd_c, v_new)
        return S, o

    S0 = jnp.zeros((Bsz, H, Dk, Dv), jnp.float32)
    xs = tuple(jnp.moveaxis(t, 1, 0) for t in (u, w, q_dec, k_dec, attn, g_last))
    _, o = lax.scan(step, S0, xs)
    o = jnp.transpose(o, (1, 0, 3, 2, 4))
    return o.reshape(Bsz, T, H, Dv)


def hybrid_mixer(h, w_in, conv_qkv_w, a_log, dt_bias, dn_norm_w, sc_conv_w, w_out):
    Bsz, L, _ = h.shape
    proj = h @ w_in.astype(h.dtype)
    idx = list(np.cumsum(IN_SIZES)[:-1])
    qkv, z, b_raw, a_raw, sc_b, sc_c, sc_h = jnp.split(proj, idx, axis=-1)

    qkv = jax.nn.silu(depthwise_conv_centred(qkv, conv_qkv_w))
    q, k, v = jnp.split(qkv, 3, axis=-1)
    q = l2norm(q.reshape(Bsz, L, DN_HEADS, DN_HEAD_DIM))
    k = l2norm(k.reshape(Bsz, L, DN_HEADS, DN_HEAD_DIM))
    v = v.reshape(Bsz, L, DN_HEADS, DN_HEAD_DIM).astype(jnp.float32)
    beta = jax.nn.sigmoid(b_raw.astype(jnp.float32)).reshape(Bsz, L, 2, DN_HEADS)
    g = -jnp.exp(a_log.astype(jnp.float32)) * jax.nn.softplus(
        a_raw.astype(jnp.float32).reshape(Bsz, L, 2, DN_HEADS) + dt_bias.astype(jnp.float32))

    def lpad(t):
        return jnp.pad(t, ((0, 0), (N_PAD, 0)) + ((0, 0),) * (t.ndim - 2))

    def flip(t):
        return jnp.flip(t, axis=1)

    qp, kp, vp, gp, bp = lpad(q), lpad(k), lpad(v), lpad(g), lpad(beta)
    o_fwd = chunk_gated_delta_rule(qp, kp, vp, gp[:, :, 0], bp[:, :, 0])
    o_bwd = flip(chunk_gated_delta_rule(flip(qp), flip(kp), flip(vp),
                                        flip(gp[:, :, 1]), flip(bp[:, :, 1])))
    o = (o_fwd + o_bwd)[:, N_PAD:]
    o = o * lax.rsqrt(jnp.mean(o * o, axis=-1, keepdims=True) + RMS_EPS)
    zf = z.astype(jnp.float32).reshape(Bsz, L, DN_HEADS, DN_HEAD_DIM)
    o = (o * dn_norm_w.astype(jnp.float32) * jax.nn.silu(zf)).reshape(Bsz, L, DN_WIDTH)
    o = o.astype(h.dtype)

    y_sc = sc_b * depthwise_conv_centred(sc_c * sc_h, sc_conv_w)

    mix = jnp.concatenate([o, y_sc], axis=-1)
    return mix @ w_out.astype(h.dtype)


def swiglu(h, w_gate_up, w_down):
    gate, up = jnp.split(h @ w_gate_up.astype(h.dtype), 2, axis=-1)
    return (jax.nn.silu(gate) * up) @ w_down.astype(h.dtype)


def setup_inputs(seed: int = 0) -> dict:
    key = jax.random.key(seed)
    ks = jax.random.split(key, 16)
    f32 = jnp.float32
    x = jax.random.normal(ks[0], (BATCH, SEQ, D_MODEL), f32)
    meta_tokens = jax.random.normal(ks[1], (N_META, D_MODEL), f32)
    norm1_w = 1.0 + 0.01 * jax.random.normal(ks[2], (DEPTH, D_MODEL), f32)
    w_in = jax.random.normal(ks[3], (DEPTH, D_MODEL, IN_WIDTH), f32) * D_MODEL ** -0.5
    conv_qkv_w = jax.random.normal(ks[4], (DEPTH, DN_CONV, 3 * DN_WIDTH), f32) * DN_CONV ** -0.5
    a_log = jnp.log(jax.random.uniform(ks[5], (DEPTH, 2, DN_HEADS), f32, minval=1.0, maxval=16.0))
    dt = jnp.exp(jax.random.uniform(ks[6], (DEPTH, 2, DN_HEADS), f32,
                                    minval=math.log(1e-3), maxval=math.log(1e-1)))
    dt_bias = dt + jnp.log(-jnp.expm1(-dt))
    dn_norm_w = 1.0 + 0.01 * jax.random.normal(ks[7], (DEPTH, DN_HEAD_DIM), f32)
    sc_conv_w = jax.random.normal(ks[8], (DEPTH, SC_CONV, SC_WIDTH), f32) * SC_CONV ** -0.5
    w_out = jax.random.normal(ks[9], (DEPTH, MIX_WIDTH, D_MODEL), f32) * MIX_WIDTH ** -0.5
    norm2_w = 1.0 + 0.01 * jax.random.normal(ks[10], (DEPTH, D_MODEL), f32)
    w_gate_up = jax.random.normal(ks[11], (DEPTH, D_MODEL, 2 * D_FF), f32) * D_MODEL ** -0.5
    w_down = jax.random.normal(ks[12], (DEPTH, D_FF, D_MODEL), f32) * D_FF ** -0.5
    final_norm_w = 1.0 + 0.01 * jax.random.normal(ks[13], (D_MODEL,), f32)
    return {"x": x, "meta_tokens": meta_tokens, "norm1_w": norm1_w, "w_in": w_in,
            "conv_qkv_w": conv_qkv_w, "a_log": a_log, "dt_bias": dt_bias,
            "dn_norm_w": dn_norm_w, "sc_conv_w": sc_conv_w, "w_out": w_out,
            "norm2_w": norm2_w, "w_gate_up": w_gate_up, "w_down": w_down,
            "final_norm_w": final_norm_w}


def reference(x, meta_tokens, norm1_w, w_in, conv_qkv_w, a_log, dt_bias, dn_norm_w,
              sc_conv_w, w_out, norm2_w, w_gate_up, w_down, final_norm_w):
    Bsz = x.shape[0]
    meta = jnp.broadcast_to(meta_tokens.astype(x.dtype)[None], (Bsz, N_META, D_MODEL))
    h = jnp.concatenate([meta, x], axis=1)
    for l in range(DEPTH):
        h = h + hybrid_mixer(rmsnorm(h, norm1_w[l]), w_in[l], conv_qkv_w[l], a_log[l],
                             dt_bias[l], dn_norm_w[l], sc_conv_w[l], w_out[l])
        h = h + swiglu(rmsnorm(h, norm2_w[l]), w_gate_up[l], w_down[l])
    h = rmsnorm(h, final_norm_w)
    return h[:, N_META:]
```

```python
import functools

import jax
import jax.numpy as jnp
from jax import lax
from jax.experimental import pallas as pl
from jax.experimental.pallas import tpu as pltpu

N_META = 16
CHUNK = 64
N_PAD = (-N_META) % CHUNK
HEADS = 4
HEAD_DIM = 128
DN_WIDTH = HEADS * HEAD_DIM
RMS_EPS = 1e-6
L2_EPS = 1e-6
HALO = 16
FF_BLOCK = 256
GATE_LANES = 128
MAX_ROW_TILE = 768
V7X_VMEM_LIMIT_BYTES = 56 * 1024 * 1024

F32 = jnp.float32
BF16 = jnp.bfloat16


def _row_tile(rows_per_seq):
    best = CHUNK
    for t in range(CHUNK, MAX_ROW_TILE + 1, CHUNK):
        if rows_per_seq % t == 0:
            best = t
    return best


def _mm(a, b):
    return jnp.dot(a.astype(BF16), b.astype(BF16), preferred_element_type=F32)


def _mm_nt(a, b):
    return lax.dot_general(a.astype(BF16), b.astype(BF16), (((1,), (1,)), ((), ())),
                           preferred_element_type=F32)


def _mm_tn(a, b):
    return lax.dot_general(a.astype(BF16), b.astype(BF16), (((0,), (0,)), ((), ())),
                           preferred_element_type=F32)


def _split3(x):
    hi = x.astype(BF16)
    r1 = x - hi.astype(F32)
    mid = r1.astype(BF16)
    lo = (r1 - mid.astype(F32)).astype(BF16)
    return hi, mid, lo


def _mask_dot(mask01, x):
    m = mask01.astype(BF16)
    hi, mid, lo = _split3(x)
    d = lambda p: jnp.dot(m, p, preferred_element_type=F32)
    return d(hi) + (d(mid) + d(lo))


def _dot_mask(x, mask01):
    m = mask01.astype(BF16)
    hi, mid, lo = _split3(x)
    d = lambda p: jnp.dot(p, m, preferred_element_type=F32)
    return d(hi) + (d(mid) + d(lo))


def _silu(x):
    return x * jax.nn.sigmoid(x)


def _softplus(x):
    return jnp.maximum(x, 0.0) + jnp.log1p(jnp.exp(-jnp.abs(x)))


def _inproj_kernel(hprev_ref, h_ref, hnext_ref, nw_ref, wqkv_ref, wz_ref, wsc_ref, wba_ref,
                   cw_ref, scw_ref,
                   q_ref, k_ref, v_ref, z_ref, ysc_ref, ba_ref,
                   hn_scr, pq_scr, ps_scr, *, tm, n_tiles):
    j = pl.program_id(1)
    nw = nw_ref[...]

    def norm(x):
        ms = jnp.mean(x * x, axis=-1, keepdims=True)
        return (x * lax.rsqrt(ms + RMS_EPS) * nw).astype(BF16)

    nxt = jnp.where(j == n_tiles - 1, 0.0, hnext_ref[0])
    hn_scr[0:HALO, :] = norm(hprev_ref[0])
    hn_scr[HALO:HALO + tm, :] = norm(h_ref[0])
    hn_scr[HALO + tm:, :] = norm(nxt)

    pq_scr[...] = jnp.dot(hn_scr[...], wqkv_ref[...], preferred_element_type=F32)
    ps_scr[...] = jnp.dot(hn_scr[...], wsc_ref[...], preferred_element_type=F32)
    hm = hn_scr[HALO:HALO + tm, :]
    z_ref[0] = jnp.dot(hm, wz_ref[...], preferred_element_type=F32).astype(BF16)
    ba_ref[0] = jnp.dot(hm, wba_ref[...], preferred_element_type=F32)

    qkv_refs = (q_ref, k_ref, v_ref)

    def conv3(win, w):
        return w[0:1] * win[7:7 + CHUNK] + w[1:2] * win[8:8 + CHUNK] + w[2:3] * win[9:9 + CHUNK]

    def body(r, carry):
        s = pl.multiple_of(r * CHUNK, CHUNK)
        row = j * tm + s + lax.broadcasted_iota(jnp.int32, (CHUNK, 1), 0)
        keep = row >= N_PAD
        for g in range(3 * HEADS):
            lanes = slice(g * HEAD_DIM, (g + 1) * HEAD_DIM)
            win = pq_scr[pl.ds(s + HALO - 8, CHUNK + 16), lanes]
            y = _silu(conv3(win, cw_ref[:, lanes]))
            if g < 2 * HEADS:
                y = y * lax.rsqrt(jnp.sum(y * y, axis=-1, keepdims=True) + L2_EPS)
            if g < HEADS:
                y = y * (HEAD_DIM ** -0.5)
            y = jnp.where(keep, y, 0.0)
            out_lanes = slice((g % HEADS) * HEAD_DIM, (g % HEADS + 1) * HEAD_DIM)
            qkv_refs[g // HEADS][0, pl.ds(s, CHUNK), out_lanes] = y.astype(BF16)
        for g in range(HEADS):
            lanes = slice(g * HEAD_DIM, (g + 1) * HEAD_DIM)
            lanes_c = slice(DN_WIDTH + g * HEAD_DIM, DN_WIDTH + (g + 1) * HEAD_DIM)
            lanes_h = slice(2 * DN_WIDTH + g * HEAD_DIM, 2 * DN_WIDTH + (g + 1) * HEAD_DIM)
            gate = ps_scr[pl.ds(s + HALO, CHUNK), lanes]
            p = (ps_scr[pl.ds(s + HALO - 8, CHUNK + 16), lanes_c]
                 * ps_scr[pl.ds(s + HALO - 8, CHUNK + 16), lanes_h])
            ysc_ref[0, pl.ds(s, CHUNK), lanes] = (gate * conv3(p, scw_ref[:, lanes])).astype(BF16)
        return carry

    lax.fori_loop(0, tm // CHUNK, body, 0)


def _inproj(h, nw, wqkv, wz, wsc, wba, cw, scw, *, tm):
    bsz, lp, d = h.shape
    n_tiles = lp // tm
    hb = tm // HALO
    const = lambda shape: pl.BlockSpec(shape, lambda b, j: (0,) * len(shape),
                                       pipeline_mode=pl.Buffered(1))
    row_out = lambda w: pl.BlockSpec((1, tm, w), lambda b, j: (b, j, 0))
    sc_w = wsc.shape[1]
    return pl.pallas_call(
        functools.partial(_inproj_kernel, tm=tm, n_tiles=n_tiles),
        grid=(bsz, n_tiles),
        in_specs=[
            pl.BlockSpec((1, HALO, d), lambda b, j: (b, jnp.maximum(j * hb - 1, 0), 0)),
            pl.BlockSpec((1, tm, d), lambda b, j: (b, j, 0)),
            pl.BlockSpec((1, HALO, d), lambda b, j: (b, jnp.minimum((j + 1) * hb, lp // HALO - 1), 0)),
            const((1, d)),
            const(wqkv.shape), const(wz.shape), const(wsc.shape), const(wba.shape),
            const(cw.shape), const(scw.shape),
        ],
        out_specs=[row_out(DN_WIDTH), row_out(DN_WIDTH), row_out(DN_WIDTH), row_out(DN_WIDTH),
                   row_out(sc_w // 3), row_out(GATE_LANES)],
        out_shape=[jax.ShapeDtypeStruct((bsz, lp, DN_WIDTH), BF16)] * 4
        + [jax.ShapeDtypeStruct((bsz, lp, sc_w // 3), BF16),
           jax.ShapeDtypeStruct((bsz, lp, GATE_LANES), F32)],
        scratch_shapes=[pltpu.VMEM((tm + 2 * HALO, d), BF16),
                        pltpu.VMEM((tm + 2 * HALO, 3 * DN_WIDTH), F32),
                        pltpu.VMEM((tm + 2 * HALO, sc_w), F32)],
        compiler_params=pltpu.CompilerParams(
            dimension_semantics=("arbitrary", "arbitrary"),
            vmem_limit_bytes=V7X_VMEM_LIMIT_BYTES),
        name="inproj",
    )(h, h, h, nw, wqkv, wz, wsc, wba, cw, scw)


def _unit_tri_inverses(ms, ii, jj):
    eye = (ii == jj).astype(F32)
    same8 = (ii >> 3) == (jj >> 3)
    n1 = [jnp.where(same8, m, 0.0) for m in ms]
    n2 = [_mm(n, n) for n in n1]
    n4 = [_mm(n, n) for n in n2]
    y = [(eye - a) + _mm(eye - a, b) for a, b in zip(n1, n2)]
    x = [a + _mm(a, b) for a, b in zip(y, n4)]
    for ls in (3, 4, 5):
        pair = ((ii >> (ls + 1)) == (jj >> (ls + 1))) & ((ii >> ls) != (jj >> ls))
        t = [_mm(a, jnp.where(pair, m, 0.0)) for a, m in zip(x, ms)]
        x = [a - _mm(b, a) for a, b in zip(x, t)]
    return x


def _gdn_kernel(q_ref, k_ref, v_ref, ba_ref, art_ref, alr_ref, alc_ref, dtr_ref, dtc_ref,
                o_ref, beta_scr, gcol_scr, grow_scr, s_scr, *, n_chunks):
    ii = lax.broadcasted_iota(jnp.int32, (CHUNK, CHUNK), 0)
    jj = lax.broadcasted_iota(jnp.int32, (CHUNK, CHUNK), 1)
    lower01 = (jj <= ii).astype(F32)
    upper01 = (jj >= ii).astype(F32)
    lane = lax.broadcasted_iota(jnp.int32, (CHUNK, GATE_LANES), 1)
    sub8 = lax.broadcasted_iota(jnp.int32, (2 * HEADS, CHUNK), 0)

    def gates(c, carry):
        r0 = pl.multiple_of(c * CHUNK, CHUNK)
        ba = ba_ref[0, pl.ds(r0, CHUNK), :]
        beta_scr[pl.ds(r0, CHUNK), :] = jax.nn.sigmoid(ba)
        a_shift = pltpu.roll(ba, GATE_LANES - 2 * HEADS, axis=1)
        g = -jnp.exp(alr_ref[...]) * _softplus(a_shift + dtr_ref[...])
        g_cum = jnp.where(lane < HEADS, _mask_dot(lower01, g), _mask_dot(upper01, g))
        gcol_scr[pl.ds(r0, CHUNK), :] = g_cum
        g_row = -jnp.exp(alc_ref[...]) * _softplus(art_ref[0, c] + dtc_ref[...])
        grow_scr[c] = jnp.where(sub8 < HEADS, _dot_mask(g_row, upper01), _dot_mask(g_row, lower01))
        return carry

    lax.fori_loop(0, n_chunks, gates, 0)

    s_scr[...] = jnp.zeros(s_scr.shape, F32)
    o_ref[...] = jnp.zeros(o_ref.shape, F32)

    chains = [(d, h) for d in range(2) for h in range(HEADS)]

    def body(i, carry):
        chunk = (i, n_chunks - 1 - i)
        rows = [pl.ds(pl.multiple_of(c * CHUNK, CHUNK), CHUNK) for c in chunk]
        g_all = [gcol_scr[r, :] for r in rows]
        b_all = [beta_scr[r, :] for r in rows]
        g_rows = [grow_scr[c] for c in chunk]
        incl = (jj <= ii, jj >= ii)
        strict = (jj < ii, jj > ii)
        last = (CHUNK - 1, 0)

        q_c, k_c, v_c, g_c, b_c, decay = [], [], [], [], [], []
        for d, h in chains:
            col = d * HEADS + h
            lanes = slice(h * HEAD_DIM, (h + 1) * HEAD_DIM)
            q_c.append(q_ref[0, rows[d], lanes])
            k_c.append(k_ref[0, rows[d], lanes])
            v_c.append(v_ref[0, rows[d], lanes].astype(F32))
            g_c.append(g_all[d][:, col:col + 1])
            b_c.append(b_all[d][:, col:col + 1])
            g_r = g_rows[d][col:col + 1, :]
            decay.append(jnp.exp(jnp.where(incl[d], g_c[-1] - g_r, -jnp.inf)))
        n = len(chains)
        kk = [_mm_nt(k_c[a], k_c[a]) for a in range(n)]
        qk = [_mm_nt(q_c[a], k_c[a]) for a in range(n)]
        m = [jnp.where(strict[chains[a][0]], b_c[a] * kk[a] * decay[a], 0.0) for a in range(n)]
        attn = [qk[a] * decay[a] for a in range(n)]
        x = _unit_tri_inverses(m, ii, jj)
        e_g = [jnp.exp(g) for g in g_c]
        k_f = [k.astype(F32) for k in k_c]
        u = [_mm(x[a], v_c[a] * b_c[a]) for a in range(n)]
        w = [_mm(x[a], k_f[a] * (b_c[a] * e_g[a])) for a in range(n)]
        s = [s_scr[d * HEADS + h] for d, h in chains]
        ws = [_mm(w[a], s[a]) for a in range(n)]
        qs = [_mm(q_c[a].astype(F32) * e_g[a], s[a]) for a in range(n)]
        v_new = [u[a] - ws[a] for a in range(n)]
        g_last = [g_c[a][last[chains[a][0]]:last[chains[a][0]] + 1, :] for a in range(n)]
        k_dec = [k_f[a] * jnp.exp(g_last[a] - g_c[a]) for a in range(n)]
        av = [_mm(attn[a], v_new[a]) for a in range(n)]
        kv = [_mm_tn(k_dec[a], v_new[a]) for a in range(n)]
        for a, (d, h) in enumerate(chains):
            lanes = slice(h * HEAD_DIM, (h + 1) * HEAD_DIM)
            s_scr[d * HEADS + h] = s[a] * jnp.exp(g_last[a]) + kv[a]
            o_ref[0, rows[d], lanes] += qs[a] + av[a]
        return carry

    lax.fori_loop(0, n_chunks, body, 0)


def _gdn(q, k, v, ba, art, alr, alc, dtr, dtc):
    bsz, lp, _ = q.shape
    n_chunks = lp // CHUNK
    seq = lambda w: pl.BlockSpec((1, lp, w), lambda b: (b, 0, 0))
    small = lambda a: pl.BlockSpec(a.shape, lambda b: (0,) * a.ndim)
    return pl.pallas_call(
        functools.partial(_gdn_kernel, n_chunks=n_chunks),
        grid=(bsz,),
        in_specs=[seq(DN_WIDTH), seq(DN_WIDTH), seq(DN_WIDTH), seq(GATE_LANES),
                  pl.BlockSpec((1, n_chunks, 2 * HEADS, CHUNK), lambda b: (b, 0, 0, 0)),
                  small(alr), small(alc), small(dtr), small(dtc)],
        out_specs=seq(DN_WIDTH),
        out_shape=jax.ShapeDtypeStruct((bsz, lp, DN_WIDTH), F32),
        scratch_shapes=[pltpu.VMEM((lp, GATE_LANES), F32),
                        pltpu.VMEM((lp, GATE_LANES), F32),
                        pltpu.VMEM((n_chunks, 2 * HEADS, CHUNK), F32),
                        pltpu.VMEM((2 * HEADS, HEAD_DIM, HEAD_DIM), F32)],
        compiler_params=pltpu.CompilerParams(
            dimension_semantics=("arbitrary",),
            vmem_limit_bytes=V7X_VMEM_LIMIT_BYTES),
        name="gdn",
    )(q, k, v, ba, art, alr, alc, dtr, dtc)


def _outffn_kernel(h_ref, o_ref, z_ref, ysc_ref, dnw_ref, wo_ref, wsc_ref, n2_ref,
                   wg_ref, wu_ref, wd_ref, fw_ref, out_ref, og_scr, hn_scr, *, final):
    dnw = dnw_ref[...]
    for hd in range(HEADS):
        lanes = slice(hd * HEAD_DIM, (hd + 1) * HEAD_DIM)
        o = o_ref[:, lanes]
        o = o * lax.rsqrt(jnp.mean(o * o, axis=-1, keepdims=True) + RMS_EPS)
        og_scr[:, lanes] = (o * dnw[:, lanes] * _silu(z_ref[:, lanes].astype(F32))).astype(BF16)

    h1 = (h_ref[...]
          + jnp.dot(og_scr[...], wo_ref[...], preferred_element_type=F32)
          + jnp.dot(ysc_ref[...], wsc_ref[...], preferred_element_type=F32))
    out_ref[...] = h1
    ms = jnp.mean(h1 * h1, axis=-1, keepdims=True)
    hn_scr[...] = (h1 * lax.rsqrt(ms + RMS_EPS) * n2_ref[...]).astype(BF16)

    d_ff = wg_ref.shape[1]
    for f in range(d_ff // FF_BLOCK):
        cols = slice(f * FF_BLOCK, (f + 1) * FF_BLOCK)
        gate = jnp.dot(hn_scr[...], wg_ref[:, cols], preferred_element_type=F32)
        up = jnp.dot(hn_scr[...], wu_ref[:, cols], preferred_element_type=F32)
        act = (_silu(gate) * up).astype(BF16)
        out_ref[...] += jnp.dot(act, wd_ref[cols, :], preferred_element_type=F32)

    if final:
        h2 = out_ref[...]
        ms2 = jnp.mean(h2 * h2, axis=-1, keepdims=True)
        out_ref[...] = h2 * lax.rsqrt(ms2 + RMS_EPS) * fw_ref[...]


def _outffn(h, o, z, ysc, dnw, wo, wsc, n2, wg, wu, wd, fw, *, tm, final):
    rows, d = h.shape
    assert wg.shape[1] % FF_BLOCK == 0
    const = lambda a: pl.BlockSpec(a.shape, lambda i: (0,) * a.ndim, pipeline_mode=pl.Buffered(1))
    row = lambda w: pl.BlockSpec((tm, w), lambda i: (i, 0))
    return pl.pallas_call(
        functools.partial(_outffn_kernel, final=final),
        grid=(rows // tm,),
        in_specs=[row(d), row(DN_WIDTH), row(DN_WIDTH), row(ysc.shape[1]),
                  const(dnw), const(wo), const(wsc), const(n2),
                  const(wg), const(wu), const(wd), const(fw)],
        out_specs=row(d),
        out_shape=jax.ShapeDtypeStruct((rows, d), F32),
        scratch_shapes=[pltpu.VMEM((tm, DN_WIDTH), BF16), pltpu.VMEM((tm, d), BF16)],
        compiler_params=pltpu.CompilerParams(
            dimension_semantics=("arbitrary",),
            vmem_limit_bytes=V7X_VMEM_LIMIT_BYTES),
        name="outffn",
    )(h, o, z, ysc, dnw, wo, wsc, n2, wg, wu, wd, fw)


def kernel(x, meta_tokens, norm1_w, w_in, conv_qkv_w, a_log, dt_bias, dn_norm_w, sc_conv_w, w_out,
           norm2_w, w_gate_up, w_down, final_norm_w):
    bsz, seq, d = x.shape
    depth = w_in.shape[0]
    sc_width = sc_conv_w.shape[-1]
    d_ff = w_down.shape[1]
    assert conv_qkv_w.shape[-1] == 3 * DN_WIDTH and dn_norm_w.shape[-1] == HEAD_DIM
    assert w_in.shape[-1] == 4 * DN_WIDTH + 4 * HEADS + 3 * sc_width
    assert (N_PAD + N_META + seq) % CHUNK == 0
    lp = N_PAD + N_META + seq
    n_chunks = lp // CHUNK
    tm = _row_tile(lp)

    meta = jnp.broadcast_to(meta_tokens.astype(x.dtype)[None], (bsz, N_META, d))
    h = jnp.concatenate([jnp.zeros((bsz, N_PAD, d), x.dtype), meta, x], axis=1)

    gate0 = 4 * DN_WIDTH
    sc0 = gate0 + 4 * HEADS
    for l in range(depth):
        wqkv = w_in[l, :, :3 * DN_WIDTH].astype(BF16)
        wz = w_in[l, :, 3 * DN_WIDTH:gate0].astype(BF16)
        wba = jnp.pad(w_in[l, :, gate0:sc0], ((0, 0), (0, GATE_LANES - 4 * HEADS))).astype(BF16)
        wsc = w_in[l, :, sc0:].astype(BF16)
        q, k, v, z, ysc, ba = _inproj(h, norm1_w[l][None], wqkv, wz, wsc, wba,
                                      conv_qkv_w[l], sc_conv_w[l], tm=tm)

        art = ba[:, :, 2 * HEADS:4 * HEADS].reshape(bsz, n_chunks, CHUNK, 2 * HEADS)
        art = jnp.swapaxes(art, 2, 3)
        al = a_log[l].reshape(1, 2 * HEADS).astype(F32)
        dtb = dt_bias[l].reshape(1, 2 * HEADS).astype(F32)
        pad_row = lambda t: jnp.pad(t, ((0, 0), (0, GATE_LANES - 2 * HEADS)))
        o = _gdn(q, k, v, ba, art, pad_row(al), al.reshape(2 * HEADS, 1),
                 pad_row(dtb), dtb.reshape(2 * HEADS, 1))

        flat = lambda t: t.reshape(bsz * lp, t.shape[-1])
        h = _outffn(flat(h), flat(o), flat(z), flat(ysc),
                    jnp.tile(dn_norm_w[l], HEADS)[None].astype(F32),
                    w_out[l, :DN_WIDTH].astype(BF16), w_out[l, DN_WIDTH:].astype(BF16),
                    norm2_w[l][None],
                    w_gate_up[l, :, :d_ff].astype(BF16), w_gate_up[l, :, d_ff:].astype(BF16),
                    w_down[l].astype(BF16), final_norm_w[None],
                    tm=tm, final=(l == depth - 1)).reshape(bsz, lp, d)

    return h[:, N_PAD + N_META:]
```

```python
import functools

import jax
import jax.numpy as jnp
from jax import lax
from jax.experimental import pallas as pl
from jax.experimental.pallas import tpu as pltpu

N_META = 16
CHUNK = 64
N_PAD = (-N_META) % CHUNK
HEADS = 4
HEAD_DIM = 128
DN_WIDTH = HEADS * HEAD_DIM
RMS_EPS = 1e-6
L2_EPS = 1e-6
HALO = 16
MXU_COLS = 256
FF_BLOCK = MXU_COLS
GATE_LANES = 128
MAX_ROW_TILE = 768
MAX_PREP_GROUP = 3
V7X_VMEM_LIMIT_BYTES = 56 * 1024 * 1024

F32 = jnp.float32
BF16 = jnp.bfloat16


def _row_tile(rows_per_seq):
    best = CHUNK
    for t in range(CHUNK, MAX_ROW_TILE + 1, CHUNK):
        if rows_per_seq % t == 0:
            best = t
    return best


def _prep_group(n_chunks):
    return max(g for g in range(1, MAX_PREP_GROUP + 1) if n_chunks % g == 0)


def _mm(a, b):
    return jnp.dot(a.astype(BF16), b.astype(BF16), preferred_element_type=F32)


def _mm_nt(a, b):
    return lax.dot_general(a.astype(BF16), b.astype(BF16), (((1,), (1,)), ((), ())),
                           preferred_element_type=F32)


def _mm_tn(a, b):
    return lax.dot_general(a.astype(BF16), b.astype(BF16), (((0,), (0,)), ((), ())),
                           preferred_element_type=F32)


def _split3(x):
    hi = x.astype(BF16)
    r1 = x - hi.astype(F32)
    mid = r1.astype(BF16)
    lo = (r1 - mid.astype(F32)).astype(BF16)
    return hi, mid, lo


def _mask_dot(mask01, x):
    m = mask01.astype(BF16)
    hi, mid, lo = _split3(x)
    d = lambda p: jnp.dot(m, p, preferred_element_type=F32)
    return d(hi) + (d(mid) + d(lo))


def _dot_mask(x, mask01):
    m = mask01.astype(BF16)
    hi, mid, lo = _split3(x)
    d = lambda p: jnp.dot(p, m, preferred_element_type=F32)
    return d(hi) + (d(mid) + d(lo))


def _silu(x):
    return x * jax.nn.sigmoid(x)


def _softplus(x):
    return jnp.maximum(x, 0.0) + jnp.log1p(jnp.exp(-jnp.abs(x)))


def _inproj_kernel(hprev_ref, h_ref, hnext_ref, nw_ref, wqkv_ref, wz_ref, wsc_ref, wba_ref,
                   cw_ref, scw_ref,
                   q_ref, k_ref, v_ref, z_ref, ysc_ref, ba_ref,
                   hn_scr, pq_scr, ps_scr, *, tm, n_tiles):
    j = pl.program_id(1)
    nw = nw_ref[...]

    def norm(x):
        ms = jnp.mean(x * x, axis=-1, keepdims=True)
        return (x * lax.rsqrt(ms + RMS_EPS) * nw).astype(BF16)

    nxt = jnp.where(j == n_tiles - 1, 0.0, hnext_ref[0])
    hn_scr[0:HALO, :] = norm(hprev_ref[0])
    hn_scr[HALO:HALO + tm, :] = norm(h_ref[0])
    hn_scr[HALO + tm:, :] = norm(nxt)

    qkv_refs = (q_ref, k_ref, v_ref)
    sc_width = scw_ref.shape[1]

    def conv3(win, w):
        return w[0:1] * win[7:7 + CHUNK] + w[1:2] * win[8:8 + CHUNK] + w[2:3] * win[9:9 + CHUNK]

    def window(scr, s, lanes):
        return scr[s + HALO - 8:s + HALO + CHUNK + 8, lanes]

    def qkv_epilogue(g):
        lanes = slice(g * HEAD_DIM, (g + 1) * HEAD_DIM)
        out_lanes = slice((g % HEADS) * HEAD_DIM, (g % HEADS + 1) * HEAD_DIM)
        for s in range(0, tm, CHUNK):
            y = _silu(conv3(window(pq_scr, s, lanes), cw_ref[:, lanes]))
            if g < 2 * HEADS:
                y = y * lax.rsqrt(jnp.sum(y * y, axis=-1, keepdims=True) + L2_EPS)
            if g < HEADS:
                y = y * (HEAD_DIM ** -0.5)
            if s < N_PAD:
                row = j * tm + s + lax.broadcasted_iota(jnp.int32, (CHUNK, 1), 0)
                y = jnp.where(row >= N_PAD, y, 0.0)
            qkv_refs[g // HEADS][0, s:s + CHUNK, out_lanes] = y.astype(BF16)

    def sc_epilogue(g):
        lanes = slice(g * HEAD_DIM, (g + 1) * HEAD_DIM)
        lanes_c = slice(sc_width + g * HEAD_DIM, sc_width + (g + 1) * HEAD_DIM)
        lanes_h = slice(2 * sc_width + g * HEAD_DIM, 2 * sc_width + (g + 1) * HEAD_DIM)
        for s in range(0, tm, CHUNK):
            gate = ps_scr[s + HALO:s + HALO + CHUNK, lanes]
            p = window(ps_scr, s, lanes_c) * window(ps_scr, s, lanes_h)
            ysc_ref[0, s:s + CHUNK, lanes] = (gate * conv3(p, scw_ref[:, lanes])).astype(BF16)

    for blk in range(3 * DN_WIDTH // MXU_COLS):
        cols = slice(blk * MXU_COLS, (blk + 1) * MXU_COLS)
        pq_scr[:, cols] = jnp.dot(hn_scr[...], wqkv_ref[:, cols], preferred_element_type=F32)
        for g in range(blk * MXU_COLS // HEAD_DIM, (blk + 1) * MXU_COLS // HEAD_DIM):
            qkv_epilogue(g)
    for blk in range(sc_width // MXU_COLS):
        for part in range(3):
            cols = slice(part * sc_width + blk * MXU_COLS, part * sc_width + (blk + 1) * MXU_COLS)
            ps_scr[:, cols] = jnp.dot(hn_scr[...], wsc_ref[:, cols], preferred_element_type=F32)
        for g in range(blk * MXU_COLS // HEAD_DIM, (blk + 1) * MXU_COLS // HEAD_DIM):
            sc_epilogue(g)
    hm = hn_scr[HALO:HALO + tm, :]
    z_ref[0] = jnp.dot(hm, wz_ref[...], preferred_element_type=F32).astype(BF16)
    ba_ref[0] = jnp.dot(hm, wba_ref[...], preferred_element_type=F32)


def _inproj(h, nw, wqkv, wz, wsc, wba, cw, scw, *, tm):
    bsz, lp, d = h.shape
    n_tiles = lp // tm
    hb = tm // HALO
    const = lambda shape: pl.BlockSpec(shape, lambda b, j: (0,) * len(shape),
                                       pipeline_mode=pl.Buffered(1))
    row_out = lambda w: pl.BlockSpec((1, tm, w), lambda b, j: (b, j, 0))
    sc_w = wsc.shape[1]
    return pl.pallas_call(
        functools.partial(_inproj_kernel, tm=tm, n_tiles=n_tiles),
        grid=(bsz, n_tiles),
        in_specs=[
            pl.BlockSpec((1, HALO, d), lambda b, j: (b, jnp.maximum(j * hb - 1, 0), 0)),
            pl.BlockSpec((1, tm, d), lambda b, j: (b, j, 0)),
            pl.BlockSpec((1, HALO, d), lambda b, j: (b, jnp.minimum((j + 1) * hb, lp // HALO - 1), 0)),
            const((1, d)),
            const(wqkv.shape), const(wz.shape), const(wsc.shape), const(wba.shape),
            const(cw.shape), const(scw.shape),
        ],
        out_specs=[row_out(DN_WIDTH), row_out(DN_WIDTH), row_out(DN_WIDTH), row_out(DN_WIDTH),
                   row_out(sc_w // 3), row_out(GATE_LANES)],
        out_shape=[jax.ShapeDtypeStruct((bsz, lp, DN_WIDTH), BF16)] * 4
        + [jax.ShapeDtypeStruct((bsz, lp, sc_w // 3), BF16),
           jax.ShapeDtypeStruct((bsz, lp, GATE_LANES), F32)],
        scratch_shapes=[pltpu.VMEM((tm + 2 * HALO, d), BF16),
                        pltpu.VMEM((tm + 2 * HALO, 3 * DN_WIDTH), F32),
                        pltpu.VMEM((tm + 2 * HALO, sc_w), F32)],
        compiler_params=pltpu.CompilerParams(
            dimension_semantics=("arbitrary", "arbitrary"),
            vmem_limit_bytes=V7X_VMEM_LIMIT_BYTES),
        name="inproj",
    )(h, h, h, nw, wqkv, wz, wsc, wba, cw, scw)


def _unit_tri_inverses(ms, ii, jj):
    eye = (ii == jj).astype(F32)
    same8 = (ii >> 3) == (jj >> 3)
    n1 = [jnp.where(same8, m, 0.0) for m in ms]
    n2 = [_mm(n, n) for n in n1]
    n4 = [_mm(n, n) for n in n2]
    y = [(eye - a) + _mm(eye - a, b) for a, b in zip(n1, n2)]
    x = [a + _mm(a, b) for a, b in zip(y, n4)]
    for ls in (3, 4, 5):
        pair = ((ii >> (ls + 1)) == (jj >> (ls + 1))) & ((ii >> ls) != (jj >> ls))
        t = [_mm(a, jnp.where(pair, m, 0.0)) for a, m in zip(x, ms)]
        x = [a - _mm(b, a) for a, b in zip(x, t)]
    return x


def _gdn_kernel(q_ref, k_ref, v_ref, ba_ref, art_ref, alr_ref, alc_ref, dtr_ref, dtc_ref,
                o_ref, beta_scr, gcol_scr, grow_scr, s_scr, u_scr, w_scr, attn_scr,
                *, n_chunks, group):
    ii = lax.broadcasted_iota(jnp.int32, (CHUNK, CHUNK), 0)
    jj = lax.broadcasted_iota(jnp.int32, (CHUNK, CHUNK), 1)
    lower01 = (jj <= ii).astype(F32)
    upper01 = (jj >= ii).astype(F32)
    lane = lax.broadcasted_iota(jnp.int32, (CHUNK, GATE_LANES), 1)
    sub8 = lax.broadcasted_iota(jnp.int32, (2 * HEADS, CHUNK), 0)

    def gates(c, carry):
        r0 = pl.multiple_of(c * CHUNK, CHUNK)
        ba = ba_ref[0, pl.ds(r0, CHUNK), :]
        beta_scr[pl.ds(r0, CHUNK), :] = jax.nn.sigmoid(ba)
        a_shift = pltpu.roll(ba, GATE_LANES - 2 * HEADS, axis=1)
        g = -jnp.exp(alr_ref[...]) * _softplus(a_shift + dtr_ref[...])
        g_cum = jnp.where(lane < HEADS, _mask_dot(lower01, g), _mask_dot(upper01, g))
        gcol_scr[pl.ds(r0, CHUNK), :] = g_cum
        g_row = -jnp.exp(alc_ref[...]) * _softplus(art_ref[0, c] + dtc_ref[...])
        grow_scr[c] = jnp.where(sub8 < HEADS, _dot_mask(g_row, upper01), _dot_mask(g_row, lower01))
        return carry

    lax.fori_loop(0, n_chunks, gates, 0, unroll=group)

    s_scr[...] = jnp.zeros(s_scr.shape, F32)
    o_ref[...] = jnp.zeros(o_ref.shape, F32)

    incl = (jj <= ii, jj >= ii)
    strict = (jj < ii, jj > ii)
    last = (CHUNK - 1, 0)
    head_lanes = [slice(h * HEAD_DIM, (h + 1) * HEAD_DIM) for h in range(HEADS)]

    def prep(gi, carry):
        cs = [gi * group + t for t in range(group)]
        rows = [pl.ds(pl.multiple_of(c * CHUNK, CHUNK), CHUNK) for c in cs]
        g_all = [gcol_scr[r, :] for r in rows]
        b_all = [beta_scr[r, :] for r in rows]
        g_rows = [grow_scr[c] for c in cs]
        pairs = [(t, h) for t in range(group) for h in range(HEADS)]
        q_c = [q_ref[0, rows[t], head_lanes[h]] for t, h in pairs]
        k_c = [k_ref[0, rows[t], head_lanes[h]] for t, h in pairs]
        kk = [_mm_nt(k, k) for k in k_c]
        qk = [_mm_nt(q, k) for q, k in zip(q_c, k_c)]
        chains = [(d, p) for d in range(2) for p in range(len(pairs))]
        g_c, b_c, m, attn = [], [], [], []
        for d, p in chains:
            t, h = pairs[p]
            col = d * HEADS + h
            g_c.append(g_all[t][:, col:col + 1])
            b_c.append(b_all[t][:, col:col + 1])
            g_r = g_rows[t][col:col + 1, :]
            decay = jnp.exp(jnp.where(incl[d], g_c[-1] - g_r, -jnp.inf))
            m.append(jnp.where(strict[d], b_c[-1] * kk[p] * decay, 0.0))
            attn.append(qk[p] * decay)
        x = _unit_tri_inverses(m, ii, jj)
        for a, (d, p) in enumerate(chains):
            t, h = pairs[p]
            k_f = k_c[p].astype(F32)
            v_f = v_ref[0, rows[t], head_lanes[h]].astype(F32)
            rhs = jnp.concatenate([v_f * b_c[a], k_f * (b_c[a] * jnp.exp(g_c[a]))], axis=1)
            uw = _mm(x[a], rhs)
            u_scr[d * HEADS + h, rows[t], :] = uw[:, :HEAD_DIM]
            w_scr[d * HEADS + h, rows[t], :] = uw[:, HEAD_DIM:].astype(BF16)
            attn_scr[d * HEADS + h, rows[t], :] = attn[a].astype(BF16)
        return carry

    lax.fori_loop(0, n_chunks // group, prep, 0)

    scan_chains = [(d, h) for d in range(2) for h in range(HEADS)]

    def scan(i, carry):
        chunk = (i, n_chunks - 1 - i)
        rows = [pl.ds(pl.multiple_of(c * CHUNK, CHUNK), CHUNK) for c in chunk]
        g_all = [gcol_scr[r, :] for r in rows]
        n = len(scan_chains)
        g_c = [g_all[d][:, d * HEADS + h:d * HEADS + h + 1] for d, h in scan_chains]
        g_last = [g_c[a][last[d]:last[d] + 1, :] for a, (d, h) in enumerate(scan_chains)]
        s = [s_scr[d * HEADS + h] for d, h in scan_chains]
        w = [w_scr[d * HEADS + h, rows[d], :] for d, h in scan_chains]
        q_d = [q_ref[0, rows[d], head_lanes[h]].astype(F32) * jnp.exp(g_c[a])
               for a, (d, h) in enumerate(scan_chains)]
        k_dec_t = [(k_ref[0, rows[d], head_lanes[h]].astype(F32) * jnp.exp(g_last[a] - g_c[a])).T
                   for a, (d, h) in enumerate(scan_chains)]
        ws = [_mm(w[a], s[a]) for a in range(n)]
        qs = [_mm(q_d[a], s[a]) for a in range(n)]
        v_new = [u_scr[d * HEADS + h, rows[d], :] - ws[a] for a, (d, h) in enumerate(scan_chains)]
        kv = [_mm(k_dec_t[a], v_new[a]) for a in range(n)]
        av = [_mm(attn_scr[d * HEADS + h, rows[d], :], v_new[a])
              for a, (d, h) in enumerate(scan_chains)]
        for a, (d, h) in enumerate(scan_chains):
            s_scr[d * HEADS + h] = s[a] * jnp.exp(g_last[a]) + kv[a]
            o_ref[0, rows[d], head_lanes[h]] += qs[a] + av[a]
        return carry

    lax.fori_loop(0, n_chunks, scan, 0, unroll=group)


def _gdn(q, k, v, ba, art, alr, alc, dtr, dtc):
    bsz, lp, _ = q.shape
    n_chunks = lp // CHUNK
    seq = lambda w: pl.BlockSpec((1, lp, w), lambda b: (b, 0, 0))
    small = lambda a: pl.BlockSpec(a.shape, lambda b: (0,) * a.ndim)
    return pl.pallas_call(
        functools.partial(_gdn_kernel, n_chunks=n_chunks, group=_prep_group(n_chunks)),
        grid=(bsz,),
        in_specs=[seq(DN_WIDTH), seq(DN_WIDTH), seq(DN_WIDTH), seq(GATE_LANES),
                  pl.BlockSpec((1, n_chunks, 2 * HEADS, CHUNK), lambda b: (b, 0, 0, 0)),
                  small(alr), small(alc), small(dtr), small(dtc)],
        out_specs=seq(DN_WIDTH),
        out_shape=jax.ShapeDtypeStruct((bsz, lp, DN_WIDTH), F32),
        scratch_shapes=[pltpu.VMEM((lp, GATE_LANES), F32),
                        pltpu.VMEM((lp, GATE_LANES), F32),
                        pltpu.VMEM((n_chunks, 2 * HEADS, CHUNK), F32),
                        pltpu.VMEM((2 * HEADS, HEAD_DIM, HEAD_DIM), F32),
                        pltpu.VMEM((2 * HEADS, lp, HEAD_DIM), F32),
                        pltpu.VMEM((2 * HEADS, lp, HEAD_DIM), BF16),
                        pltpu.VMEM((2 * HEADS, lp, CHUNK), BF16)],
        compiler_params=pltpu.CompilerParams(
            dimension_semantics=("arbitrary",),
            vmem_limit_bytes=V7X_VMEM_LIMIT_BYTES),
        name="gdn",
    )(q, k, v, ba, art, alr, alc, dtr, dtc)


def _outffn_kernel(h_ref, o_ref, z_ref, ysc_ref, dnw_ref, wo_ref, wsc_ref, n2_ref,
                   wg_ref, wu_ref, wd_ref, fw_ref, out_ref, og_scr, hn_scr, *, final):
    dnw = dnw_ref[...]
    for hd in range(HEADS):
        lanes = slice(hd * HEAD_DIM, (hd + 1) * HEAD_DIM)
        o = o_ref[:, lanes]
        o = o * lax.rsqrt(jnp.mean(o * o, axis=-1, keepdims=True) + RMS_EPS)
        og_scr[:, lanes] = (o * dnw[:, lanes] * _silu(z_ref[:, lanes].astype(F32))).astype(BF16)

    h1 = (h_ref[...]
          + jnp.dot(og_scr[...], wo_ref[...], preferred_element_type=F32)
          + jnp.dot(ysc_ref[...], wsc_ref[...], preferred_element_type=F32))
    out_ref[...] = h1
    ms = jnp.mean(h1 * h1, axis=-1, keepdims=True)
    hn_scr[...] = (h1 * lax.rsqrt(ms + RMS_EPS) * n2_ref[...]).astype(BF16)

    d_ff = wg_ref.shape[1]
    for f in range(d_ff // FF_BLOCK):
        cols = slice(f * FF_BLOCK, (f + 1) * FF_BLOCK)
        gate = jnp.dot(hn_scr[...], wg_ref[:, cols], preferred_element_type=F32)
        up = jnp.dot(hn_scr[...], wu_ref[:, cols], preferred_element_type=F32)
        act = (_silu(gate) * up).astype(BF16)
        out_ref[...] += jnp.dot(act, wd_ref[cols, :], preferred_element_type=F32)

    if final:
        h2 = out_ref[...]
        ms2 = jnp.mean(h2 * h2, axis=-1, keepdims=True)
        out_ref[...] = h2 * lax.rsqrt(ms2 + RMS_EPS) * fw_ref[...]


def _outffn(h, o, z, ysc, dnw, wo, wsc, n2, wg, wu, wd, fw, *, tm, final):
    rows, d = h.shape
    assert wg.shape[1] % FF_BLOCK == 0
    const = lambda a: pl.BlockSpec(a.shape, lambda i: (0,) * a.ndim, pipeline_mode=pl.Buffered(1))
    row = lambda w: pl.BlockSpec((tm, w), lambda i: (i, 0))
    return pl.pallas_call(
        functools.partial(_outffn_kernel, final=final),
        grid=(rows // tm,),
        in_specs=[row(d), row(DN_WIDTH), row(DN_WIDTH), row(ysc.shape[1]),
                  const(dnw), const(wo), const(wsc), const(n2),
                  const(wg), const(wu), const(wd), const(fw)],
        out_specs=row(d),
        out_shape=jax.ShapeDtypeStruct((rows, d), F32),
        scratch_shapes=[pltpu.VMEM((tm, DN_WIDTH), BF16), pltpu.VMEM((tm, d), BF16)],
        compiler_params=pltpu.CompilerParams(
            dimension_semantics=("arbitrary",),
            vmem_limit_bytes=V7X_VMEM_LIMIT_BYTES),
        name="outffn",
    )(h, o, z, ysc, dnw, wo, wsc, n2, wg, wu, wd, fw)


def kernel(x, meta_tokens, norm1_w, w_in, conv_qkv_w, a_log, dt_bias, dn_norm_w, sc_conv_w, w_out,
           norm2_w, w_gate_up, w_down, final_norm_w):
    bsz, seq, d = x.shape
    depth = w_in.shape[0]
    sc_width = sc_conv_w.shape[-1]
    d_ff = w_down.shape[1]
    assert conv_qkv_w.shape[-1] == 3 * DN_WIDTH and dn_norm_w.shape[-1] == HEAD_DIM
    assert w_in.shape[-1] == 4 * DN_WIDTH + 4 * HEADS + 3 * sc_width
    assert (N_PAD + N_META + seq) % CHUNK == 0
    lp = N_PAD + N_META + seq
    n_chunks = lp // CHUNK
    tm = _row_tile(lp)

    meta = jnp.broadcast_to(meta_tokens.astype(x.dtype)[None], (bsz, N_META, d))
    h = jnp.concatenate([jnp.zeros((bsz, N_PAD, d), x.dtype), meta, x], axis=1)

    gate0 = 4 * DN_WIDTH
    sc0 = gate0 + 4 * HEADS
    for l in range(depth):
        wqkv = w_in[l, :, :3 * DN_WIDTH].astype(BF16)
        wz = w_in[l, :, 3 * DN_WIDTH:gate0].astype(BF16)
        wba = jnp.pad(w_in[l, :, gate0:sc0], ((0, 0), (0, GATE_LANES - 4 * HEADS))).astype(BF16)
        wsc = w_in[l, :, sc0:].astype(BF16)
        q, k, v, z, ysc, ba = _inproj(h, norm1_w[l][None], wqkv, wz, wsc, wba,
                                      conv_qkv_w[l], sc_conv_w[l], tm=tm)

        art = ba[:, :, 2 * HEADS:4 * HEADS].reshape(bsz, n_chunks, CHUNK, 2 * HEADS)
        art = jnp.swapaxes(art, 2, 3)
        al = a_log[l].reshape(1, 2 * HEADS).astype(F32)
        dtb = dt_bias[l].reshape(1, 2 * HEADS).astype(F32)
        pad_row = lambda t: jnp.pad(t, ((0, 0), (0, GATE_LANES - 2 * HEADS)))
        o = _gdn(q, k, v, ba, art, pad_row(al), al.reshape(2 * HEADS, 1),
                 pad_row(dtb), dtb.reshape(2 * HEADS, 1))

        flat = lambda t: t.reshape(bsz * lp, t.shape[-1])
        h = _outffn(flat(h), flat(o), flat(z), flat(ysc),
                    jnp.tile(dn_norm_w[l], HEADS)[None].astype(F32),
                    w_out[l, :DN_WIDTH].astype(BF16), w_out[l, DN_WIDTH:].astype(BF16),
                    norm2_w[l][None],
                    w_gate_up[l, :, :d_ff].astype(BF16), w_gate_up[l, :, d_ff:].astype(BF16),
                    w_down[l].astype(BF16), final_norm_w[None],
                    tm=tm, final=(l == depth - 1)).reshape(bsz, lp, d)

    return h[:, N_PAD + N_META:]
```

```python
import functools
import itertools

import jax
import jax.numpy as jnp
from jax import lax
from jax.experimental import pallas as pl
from jax.experimental.pallas import tpu as pltpu

N_META = 16
CHUNK = 64
N_PAD = (-N_META) % CHUNK
HEADS = 4
HEAD_DIM = 128
DN_WIDTH = HEADS * HEAD_DIM
RMS_EPS = 1e-6
L2_EPS = 1e-6
HALO = 16
MXU_COLS = 256
FF_BLOCK = MXU_COLS
GATE_LANES = 128
MAX_ROW_TILE = 768
MAX_PREP_GROUP = 3
MAX_GATE_UNROLL = 3
V7X_VMEM_LIMIT_BYTES = 56 * 1024 * 1024

F32 = jnp.float32
BF16 = jnp.bfloat16


def _row_tile(rows_per_seq):
    best = CHUNK
    for t in range(CHUNK, MAX_ROW_TILE + 1, CHUNK):
        if rows_per_seq % t == 0:
            best = t
    return best


def _largest_divisor(n, cap):
    return max(g for g in range(1, cap + 1) if n % g == 0)


def _mm(a, b):
    return jnp.dot(a.astype(BF16), b.astype(BF16), preferred_element_type=F32)


def _mm_nt(a, b):
    return lax.dot_general(a.astype(BF16), b.astype(BF16), (((1,), (1,)), ((), ())),
                           preferred_element_type=F32)


def _split3(x):
    hi = x.astype(BF16)
    r1 = x - hi.astype(F32)
    mid = r1.astype(BF16)
    lo = (r1 - mid.astype(F32)).astype(BF16)
    return hi, mid, lo


def _mask_dot(mask01, x):
    m = mask01.astype(BF16)
    hi, mid, lo = _split3(x)
    d = lambda p: jnp.dot(m, p, preferred_element_type=F32)
    return d(hi) + (d(mid) + d(lo))


def _gate_col(d, h):
    return (h % 2) * HEADS + d * (HEADS // 2) + h // 2


def _silu(x):
    return x * jax.nn.sigmoid(x)


def _softplus(x):
    return jnp.maximum(x, 0.0) + jnp.log1p(jnp.exp(-jnp.abs(x)))


def _inproj_kernel(hprev_ref, h_ref, hnext_ref, nw_ref, wqkv_ref, wz_ref, wsc_ref, wba_ref,
                   cw_ref, scw_ref,
                   q_ref, k_ref, v_ref, z_ref, ysc_ref, ba_ref,
                   hn_scr, pq_scr, ps_scr, *, tm, n_tiles):
    j = pl.program_id(1)
    nw = nw_ref[...]

    def norm(x):
        ms = jnp.mean(x * x, axis=-1, keepdims=True)
        return (x * lax.rsqrt(ms + RMS_EPS) * nw).astype(BF16)

    nxt = jnp.where(j == n_tiles - 1, 0.0, hnext_ref[0])
    hn_scr[0:HALO, :] = norm(hprev_ref[0])
    hn_scr[HALO:HALO + tm, :] = norm(h_ref[0])
    hn_scr[HALO + tm:, :] = norm(nxt)

    qkv_refs = (q_ref, k_ref, v_ref)
    sc_width = scw_ref.shape[1]

    def conv3(win, w):
        return w[0:1] * win[7:7 + CHUNK] + w[1:2] * win[8:8 + CHUNK] + w[2:3] * win[9:9 + CHUNK]

    def window(scr, s, lanes):
        return scr[s + HALO - 8:s + HALO + CHUNK + 8, lanes]

    def qkv_epilogue(g):
        lanes = slice(g * HEAD_DIM, (g + 1) * HEAD_DIM)
        out_lanes = slice((g % HEADS) * HEAD_DIM, (g % HEADS + 1) * HEAD_DIM)
        for s in range(0, tm, CHUNK):
            y = _silu(conv3(window(pq_scr, s, lanes), cw_ref[:, lanes]))
            if g < 2 * HEADS:
                y = y * lax.rsqrt(jnp.sum(y * y, axis=-1, keepdims=True) + L2_EPS)
            if g < HEADS:
                y = y * (HEAD_DIM ** -0.5)
            if s < N_PAD:
                row = j * tm + s + lax.broadcasted_iota(jnp.int32, (CHUNK, 1), 0)
                y = jnp.where(row >= N_PAD, y, 0.0)
            qkv_refs[g // HEADS][0, s:s + CHUNK, out_lanes] = y.astype(BF16)

    def sc_epilogue(g):
        lanes = slice(g * HEAD_DIM, (g + 1) * HEAD_DIM)
        lanes_c = slice(sc_width + g * HEAD_DIM, sc_width + (g + 1) * HEAD_DIM)
        lanes_h = slice(2 * sc_width + g * HEAD_DIM, 2 * sc_width + (g + 1) * HEAD_DIM)
        for s in range(0, tm, CHUNK):
            gate = ps_scr[s + HALO:s + HALO + CHUNK, lanes]
            p = window(ps_scr, s, lanes_c) * window(ps_scr, s, lanes_h)
            ysc_ref[0, s:s + CHUNK, lanes] = (gate * conv3(p, scw_ref[:, lanes])).astype(BF16)

    for blk in range(3 * DN_WIDTH // MXU_COLS):
        cols = slice(blk * MXU_COLS, (blk + 1) * MXU_COLS)
        pq_scr[:, cols] = jnp.dot(hn_scr[...], wqkv_ref[:, cols], preferred_element_type=F32)
        for g in range(blk * MXU_COLS // HEAD_DIM, (blk + 1) * MXU_COLS // HEAD_DIM):
            qkv_epilogue(g)
    for blk in range(sc_width // MXU_COLS):
        for part in range(3):
            cols = slice(part * sc_width + blk * MXU_COLS, part * sc_width + (blk + 1) * MXU_COLS)
            ps_scr[:, cols] = jnp.dot(hn_scr[...], wsc_ref[:, cols], preferred_element_type=F32)
        for g in range(blk * MXU_COLS // HEAD_DIM, (blk + 1) * MXU_COLS // HEAD_DIM):
            sc_epilogue(g)
    hm = hn_scr[HALO:HALO + tm, :]
    z_ref[0] = jnp.dot(hm, wz_ref[...], preferred_element_type=F32).astype(BF16)
    ba_ref[0] = jnp.dot(hm, wba_ref[...], preferred_element_type=F32)


def _inproj(h, nw, wqkv, wz, wsc, wba, cw, scw, *, tm):
    bsz, lp, d = h.shape
    n_tiles = lp // tm
    hb = tm // HALO
    const = lambda shape: pl.BlockSpec(shape, lambda b, j: (0,) * len(shape),
                                       pipeline_mode=pl.Buffered(1))
    row_out = lambda w: pl.BlockSpec((1, tm, w), lambda b, j: (b, j, 0))
    sc_w = wsc.shape[1]
    return pl.pallas_call(
        functools.partial(_inproj_kernel, tm=tm, n_tiles=n_tiles),
        grid=(bsz, n_tiles),
        in_specs=[
            pl.BlockSpec((1, HALO, d), lambda b, j: (b, jnp.maximum(j * hb - 1, 0), 0)),
            pl.BlockSpec((1, tm, d), lambda b, j: (b, j, 0)),
            pl.BlockSpec((1, HALO, d), lambda b, j: (b, jnp.minimum((j + 1) * hb, lp // HALO - 1), 0)),
            const((1, d)),
            const(wqkv.shape), const(wz.shape), const(wsc.shape), const(wba.shape),
            const(cw.shape), const(scw.shape),
        ],
        out_specs=[row_out(DN_WIDTH), row_out(DN_WIDTH), row_out(DN_WIDTH), row_out(DN_WIDTH),
                   row_out(sc_w // 3), row_out(GATE_LANES)],
        out_shape=[jax.ShapeDtypeStruct((bsz, lp, DN_WIDTH), BF16)] * 4
        + [jax.ShapeDtypeStruct((bsz, lp, sc_w // 3), BF16),
           jax.ShapeDtypeStruct((bsz, lp, GATE_LANES), F32)],
        scratch_shapes=[pltpu.VMEM((tm + 2 * HALO, d), BF16),
                        pltpu.VMEM((tm + 2 * HALO, 3 * DN_WIDTH), F32),
                        pltpu.VMEM((tm + 2 * HALO, sc_w), F32)],
        compiler_params=pltpu.CompilerParams(
            dimension_semantics=("arbitrary", "arbitrary"),
            vmem_limit_bytes=V7X_VMEM_LIMIT_BYTES),
        name="inproj",
    )(h, h, h, nw, wqkv, wz, wsc, wba, cw, scw)


def _blockdiag_pair(p, same_block):
    return jnp.where(same_block, jnp.concatenate([p, p], axis=0), 0.0).astype(BF16)


def _unit_tri_inverse_stages(ms, ii, jj, same_block, out):
    mm = lambda a, b: jnp.dot(a.astype(BF16), _blockdiag_pair(b, same_block),
                              preferred_element_type=F32)
    eye = (ii == jj).astype(F32)
    same8 = (ii >> 3) == (jj >> 3)
    n1 = [jnp.where(same8, m, 0.0) for m in ms]
    n2 = [mm(n, n) for n in n1]
    yield
    both = [mm(jnp.concatenate([b, eye - a], axis=0), b) for a, b in zip(n1, n2)]
    n4 = [p[:CHUNK] for p in both]
    y = [(eye - a) + p[CHUNK:] for a, p in zip(n1, both)]
    yield
    x = [a + mm(a, b) for a, b in zip(y, n4)]
    yield
    for ls in (3, 4, 5):
        pair = ((ii >> (ls + 1)) == (jj >> (ls + 1))) & ((ii >> ls) != (jj >> ls))
        t = [mm(a, jnp.where(pair, m, 0.0)) for a, m in zip(x, ms)]
        yield
        x = [a - mm(b, a) for a, b in zip(x, t)]
        yield
    out.extend(x)


def _gdn_kernel(q_ref, k_ref, v_ref, ba_ref, alr_ref, dtr_ref,
                o_ref, beta_scr, gcol_scr, grow_scr, s_scr, u_scr, wq_scr, ka_scr,
                *, n_chunks, group, unroll):
    ii = lax.broadcasted_iota(jnp.int32, (2 * CHUNK, CHUNK), 0)
    jj = lax.broadcasted_iota(jnp.int32, (2 * CHUNK, CHUNK), 1)
    cum01 = jnp.where(ii < CHUNK, (jj <= ii).astype(F32), (jj >= ii - CHUNK).astype(F32))
    lane = lax.broadcasted_iota(jnp.int32, (CHUNK, GATE_LANES), 1)
    fwd_lane = (lane & (HEADS - 1)) < HEADS // 2

    def gates(c, carry):
        r0 = pl.multiple_of(c * CHUNK, CHUNK)
        ba = ba_ref[0, pl.ds(r0, CHUNK), :]
        beta_scr[pl.ds(r0, CHUNK), :] = jax.nn.sigmoid(ba)
        a_shift = pltpu.roll(ba, GATE_LANES - 2 * HEADS, axis=1)
        g = -jnp.exp(alr_ref[...]) * _softplus(a_shift + dtr_ref[...])
        both = _mask_dot(cum01, g)
        g_cum = jnp.where(fwd_lane, both[:CHUNK], both[CHUNK:])
        gcol_scr[pl.ds(r0, CHUNK), :] = g_cum
        g8 = g_cum.T[:2 * HEADS]
        grow_scr[c] = jnp.concatenate([g8[:HEADS], g8[HEADS:]], axis=1)
        return carry

    lax.fori_loop(0, n_chunks, gates, 0, unroll=unroll)

    s_scr[...] = jnp.zeros(s_scr.shape, F32)
    o_ref[...] = jnp.zeros(o_ref.shape, F32)

    last = (CHUNK - 1, 0)
    head_lanes = [slice(h * HEAD_DIM, (h + 1) * HEAD_DIM) for h in range(HEADS)]
    pi = lax.broadcasted_iota(jnp.int32, (CHUNK, 2 * CHUNK), 0)
    pl_lane = lax.broadcasted_iota(jnp.int32, (CHUNK, 2 * CHUNK), 1)
    pj = pl_lane & (CHUNK - 1)
    first_half = pl_lane < CHUNK
    same_block = ((lax.broadcasted_iota(jnp.int32, (2 * CHUNK, 2 * CHUNK), 0) >> 6)
                  == (lax.broadcasted_iota(jnp.int32, (2 * CHUNK, 2 * CHUNK), 1) >> 6))
    incl = (pj <= pi, pj >= pi)
    strict = (pj < pi, pj > pi)
    zeros_k = jnp.zeros((CHUNK, HEAD_DIM), BF16)
    zeros_r = jnp.zeros((CHUNK, 2 * HEAD_DIM), F32)

    n_groups = n_chunks // group
    steps = [(d, t) for d in range(2) for t in range(group)]
    pairs = [(d, t, hp) for d, t in steps for hp in range(HEADS // 2)]

    def chunk_of(gi, d, t):
        i = gi * group + t
        return i if d == 0 else n_chunks - 1 - i

    def chunk_rows(c):
        return pl.ds(pl.multiple_of(c * CHUNK, CHUNK), CHUNK)

    def slot_index(d, t, h):
        return (d * HEADS + h) * group + t

    def prep_stages(gi, slot):
        rows = {(d, t): chunk_rows(chunk_of(gi, d, t)) for d, t in steps}
        q_c, k_c, kk, qk = {}, {}, [], []
        for d, t, hp in pairs:
            ha, hb = 2 * hp, 2 * hp + 1
            for h in (ha, hb):
                q_c[d, t, h] = q_ref[0, rows[d, t], head_lanes[h]]
                k_c[d, t, h] = k_ref[0, rows[d, t], head_lanes[h]]
            ka, kb = k_c[d, t, ha], k_c[d, t, hb]
            k_bd_t = jnp.concatenate([jnp.concatenate([ka, zeros_k], axis=1),
                                      jnp.concatenate([zeros_k, kb], axis=1)], axis=0)
            kq = jnp.concatenate([jnp.concatenate([ka, kb], axis=1),
                                  jnp.concatenate([q_c[d, t, ha], q_c[d, t, hb]], axis=1)], axis=0)
            both = _mm_nt(kq, k_bd_t)
            kk.append(both[:CHUNK])
            qk.append(both[CHUNK:])
        yield
        g_all = {dt: gcol_scr[rows[dt], :] for dt in steps}
        b_all = {dt: beta_scr[rows[dt], :] for dt in steps}
        g_rows = {(d, t): grow_scr[chunk_of(gi, d, t)] for d, t in steps}
        g_c, b_c, m, attn = {}, {}, [], []
        for a, (d, t, hp) in enumerate(pairs):
            for h in (2 * hp, 2 * hp + 1):
                col = _gate_col(d, h)
                g_c[d, t, h] = g_all[d, t][:, col:col + 1]
                b_c[d, t, h] = b_all[d, t][:, col:col + 1]
            g_cp = jnp.where(first_half, g_c[d, t, 2 * hp], g_c[d, t, 2 * hp + 1])
            b_cp = jnp.where(first_half, b_c[d, t, 2 * hp], b_c[d, t, 2 * hp + 1])
            row = d * (HEADS // 2) + hp
            g_rp = g_rows[d, t][row:row + 1, :]
            decay = jnp.exp(jnp.where(incl[d], g_cp - g_rp, -jnp.inf))
            m.append(jnp.where(strict[d], b_cp * kk[a] * decay, 0.0))
            attn.append(qk[a] * decay)
        x = []
        yield from _unit_tri_inverse_stages(m, pi, pj, same_block, x)
        for a, (d, t, hp) in enumerate(pairs):
            rhs = []
            for h in (2 * hp, 2 * hp + 1):
                k_f = k_c[d, t, h].astype(F32)
                v_f = v_ref[0, rows[d, t], head_lanes[h]].astype(F32)
                e_g = jnp.exp(g_c[d, t, h])
                g_last = g_c[d, t, h][last[d]:last[d] + 1, :]
                idx = slot_index(d, t, h)
                wq_scr[slot, idx, CHUNK:, :] = (q_c[d, t, h].astype(F32) * e_g).astype(BF16)
                ka_scr[slot, idx, :HEAD_DIM, :] = (k_f * jnp.exp(g_last - g_c[d, t, h])).T.astype(BF16)
                rhs.append(jnp.concatenate([v_f * b_c[d, t, h], k_f * (b_c[d, t, h] * e_g)], axis=1))
            rhs_bd = jnp.concatenate([jnp.concatenate([rhs[0], zeros_r], axis=1),
                                      jnp.concatenate([zeros_r, rhs[1]], axis=1)], axis=0)
            uw = _mm(x[a], rhs_bd)
            attn_b = attn[a].astype(BF16)
            for n, h in enumerate((2 * hp, 2 * hp + 1)):
                idx = slot_index(d, t, h)
                u_scr[slot, idx] = uw[:, 2 * n * HEAD_DIM:(2 * n + 1) * HEAD_DIM]
                wq_scr[slot, idx, :CHUNK, :] = (
                    uw[:, (2 * n + 1) * HEAD_DIM:(2 * n + 2) * HEAD_DIM].astype(BF16))
                ka_scr[slot, idx, HEAD_DIM:, :] = attn_b[:, n * CHUNK:(n + 1) * CHUNK]
        yield

    def scan_stages(gi, slot):
        for t in range(group):
            step = [(d, h) for d in range(2) for h in range(HEADS)]
            s = [s_scr[d * HEADS + h] for d, h in step]
            wqs = [_mm(wq_scr[slot, slot_index(d, t, h)], s[a]) for a, (d, h) in enumerate(step)]
            yield
            v_new = [u_scr[slot, slot_index(d, t, h)] - wqs[a][:CHUNK]
                     for a, (d, h) in enumerate(step)]
            kav = [_mm(ka_scr[slot, slot_index(d, t, h)], v_new[a]) for a, (d, h) in enumerate(step)]
            for a, (d, h) in enumerate(step):
                c = chunk_of(gi, d, t)
                col = _gate_col(d, h)
                edge = gcol_scr[pl.ds(pl.multiple_of(c * CHUNK + (last[d] // 8) * 8, 8), 8), :]
                g_last = edge[last[d] % 8:last[d] % 8 + 1, col:col + 1]
                s_scr[d * HEADS + h] = s[a] * jnp.exp(g_last) + kav[a][:HEAD_DIM]
                o_ref[0, chunk_rows(c), head_lanes[h]] += wqs[a][CHUNK:] + kav[a][HEAD_DIM:]
            yield

    for _ in prep_stages(0, 0):
        pass

    def body(gi, carry):
        slot = gi % 2
        nxt = jnp.minimum(gi + 1, n_groups - 1)
        for _ in itertools.zip_longest(prep_stages(nxt, 1 - slot), scan_stages(gi, slot)):
            pass
        return carry

    lax.fori_loop(0, n_groups, body, 0)


def _gdn(q, k, v, ba, alr, dtr):
    bsz, lp, _ = q.shape
    n_chunks = lp // CHUNK
    seq = lambda w: pl.BlockSpec((1, lp, w), lambda b: (b, 0, 0))
    small = lambda a: pl.BlockSpec(a.shape, lambda b: (0,) * a.ndim)
    group = _largest_divisor(n_chunks, MAX_PREP_GROUP)
    n_slot = 2 * HEADS * group
    return pl.pallas_call(
        functools.partial(_gdn_kernel, n_chunks=n_chunks, group=group,
                          unroll=_largest_divisor(n_chunks, MAX_GATE_UNROLL)),
        grid=(bsz,),
        in_specs=[seq(DN_WIDTH), seq(DN_WIDTH), seq(DN_WIDTH), seq(GATE_LANES),
                  small(alr), small(dtr)],
        out_specs=seq(DN_WIDTH),
        out_shape=jax.ShapeDtypeStruct((bsz, lp, DN_WIDTH), F32),
        scratch_shapes=[pltpu.VMEM((lp, GATE_LANES), F32),
                        pltpu.VMEM((lp, GATE_LANES), F32),
                        pltpu.VMEM((n_chunks, HEADS, 2 * CHUNK), F32),
                        pltpu.VMEM((2 * HEADS, HEAD_DIM, HEAD_DIM), F32),
                        pltpu.VMEM((2, n_slot, CHUNK, HEAD_DIM), F32),
                        pltpu.VMEM((2, n_slot, 2 * CHUNK, HEAD_DIM), BF16),
                        pltpu.VMEM((2, n_slot, HEAD_DIM + CHUNK, CHUNK), BF16)],
        compiler_params=pltpu.CompilerParams(
            dimension_semantics=("arbitrary",),
            vmem_limit_bytes=V7X_VMEM_LIMIT_BYTES),
        name="gdn",
    )(q, k, v, ba, alr, dtr)


def _outffn_kernel(h_ref, o_ref, z_ref, ysc_ref, dnw_ref, wo_ref, wsc_ref, n2_ref,
                   wg_ref, wu_ref, wd_ref, fw_ref, out_ref, og_scr, hn_scr, *, final):
    dnw = dnw_ref[...]
    for hd in range(HEADS):
        lanes = slice(hd * HEAD_DIM, (hd + 1) * HEAD_DIM)
        o = o_ref[:, lanes]
        o = o * lax.rsqrt(jnp.mean(o * o, axis=-1, keepdims=True) + RMS_EPS)
        og_scr[:, lanes] = (o * dnw[:, lanes] * _silu(z_ref[:, lanes].astype(F32))).astype(BF16)

    h1 = (h_ref[...]
          + jnp.dot(og_scr[...], wo_ref[...], preferred_element_type=F32)
          + jnp.dot(ysc_ref[...], wsc_ref[...], preferred_element_type=F32))
    out_ref[...] = h1
    ms = jnp.mean(h1 * h1, axis=-1, keepdims=True)
    hn_scr[...] = (h1 * lax.rsqrt(ms + RMS_EPS) * n2_ref[...]).astype(BF16)

    d_ff = wg_ref.shape[1]
    for f in range(d_ff // FF_BLOCK):
        cols = slice(f * FF_BLOCK, (f + 1) * FF_BLOCK)
        gate = jnp.dot(hn_scr[...], wg_ref[:, cols], preferred_element_type=F32)
        up = jnp.dot(hn_scr[...], wu_ref[:, cols], preferred_element_type=F32)
        act = (_silu(gate) * up).astype(BF16)
        out_ref[...] += jnp.dot(act, wd_ref[cols, :], preferred_element_type=F32)

    if final:
        h2 = out_ref[...]
        ms2 = jnp.mean(h2 * h2, axis=-1, keepdims=True)
        out_ref[...] = h2 * lax.rsqrt(ms2 + RMS_EPS) * fw_ref[...]


def _outffn(h, o, z, ysc, dnw, wo, wsc, n2, wg, wu, wd, fw, *, tm, final):
    rows, d = h.shape
    assert wg.shape[1] % FF_BLOCK == 0
    const = lambda a: pl.BlockSpec(a.shape, lambda i: (0,) * a.ndim, pipeline_mode=pl.Buffered(1))
    row = lambda w: pl.BlockSpec((tm, w), lambda i: (i, 0))
    return pl.pallas_call(
        functools.partial(_outffn_kernel, final=final),
        grid=(rows // tm,),
        in_specs=[row(d), row(DN_WIDTH), row(DN_WIDTH), row(ysc.shape[1]),
                  const(dnw), const(wo), const(wsc), const(n2),
                  const(wg), const(wu), const(wd), const(fw)],
        out_specs=row(d),
        out_shape=jax.ShapeDtypeStruct((rows, d), F32),
        scratch_shapes=[pltpu.VMEM((tm, DN_WIDTH), BF16), pltpu.VMEM((tm, d), BF16)],
        compiler_params=pltpu.CompilerParams(
            dimension_semantics=("arbitrary",),
            vmem_limit_bytes=V7X_VMEM_LIMIT_BYTES),
        name="outffn",
    )(h, o, z, ysc, dnw, wo, wsc, n2, wg, wu, wd, fw)


def kernel(x, meta_tokens, norm1_w, w_in, conv_qkv_w, a_log, dt_bias, dn_norm_w, sc_conv_w, w_out,
           norm2_w, w_gate_up, w_down, final_norm_w):
    bsz, seq, d = x.shape
    depth = w_in.shape[0]
    sc_width = sc_conv_w.shape[-1]
    d_ff = w_down.shape[1]
    assert conv_qkv_w.shape[-1] == 3 * DN_WIDTH and dn_norm_w.shape[-1] == HEAD_DIM
    assert w_in.shape[-1] == 4 * DN_WIDTH + 4 * HEADS + 3 * sc_width
    assert (N_PAD + N_META + seq) % CHUNK == 0
    lp = N_PAD + N_META + seq
    n_chunks = lp // CHUNK
    tm = _row_tile(lp)

    meta = jnp.broadcast_to(meta_tokens.astype(x.dtype)[None], (bsz, N_META, d))
    h = jnp.concatenate([jnp.zeros((bsz, N_PAD, d), x.dtype), meta, x], axis=1)

    gate_perm = [0] * (2 * HEADS)
    for dd in range(2):
        for hh in range(HEADS):
            gate_perm[_gate_col(dd, hh)] = dd * HEADS + hh
    gate_perm = jnp.array(gate_perm, jnp.int32)
    gate0 = 4 * DN_WIDTH
    sc0 = gate0 + 4 * HEADS
    for l in range(depth):
        wqkv = w_in[l, :, :3 * DN_WIDTH].astype(BF16)
        wz = w_in[l, :, 3 * DN_WIDTH:gate0].astype(BF16)
        wba = jnp.concatenate([w_in[l][:, gate0 + gate_perm], w_in[l][:, gate0 + 2 * HEADS + gate_perm]],
                              axis=1)
        wba = jnp.pad(wba, ((0, 0), (0, GATE_LANES - 4 * HEADS))).astype(BF16)
        wsc = w_in[l, :, sc0:].astype(BF16)
        q, k, v, z, ysc, ba = _inproj(h, norm1_w[l][None], wqkv, wz, wsc, wba,
                                      conv_qkv_w[l], sc_conv_w[l], tm=tm)

        pad_row = lambda t: jnp.pad(t.reshape(2 * HEADS)[gate_perm].astype(F32)[None],
                                    ((0, 0), (0, GATE_LANES - 2 * HEADS)))
        o = _gdn(q, k, v, ba, pad_row(a_log[l]), pad_row(dt_bias[l]))

        flat = lambda t: t.reshape(bsz * lp, t.shape[-1])
        h = _outffn(flat(h), flat(o), flat(z), flat(ysc),
                    jnp.tile(dn_norm_w[l], HEADS)[None].astype(F32),
                    w_out[l, :DN_WIDTH].astype(BF16), w_out[l, DN_WIDTH:].astype(BF16),
                    norm2_w[l][None],
                    w_gate_up[l, :, :d_ff].astype(BF16), w_gate_up[l, :, d_ff:].astype(BF16),
                    w_down[l].astype(BF16), final_norm_w[None],
                    tm=tm, final=(l == depth - 1)).reshape(bsz, lp, d)

    return h[:, N_PAD + N_META:]
```

```python
import functools
import itertools

import jax
import jax.numpy as jnp
from jax import lax
from jax.experimental import pallas as pl
from jax.experimental.pallas import tpu as pltpu

N_META = 16
CHUNK = 64
N_PAD = (-N_META) % CHUNK
HEADS = 4
HEAD_DIM = 128
DN_WIDTH = HEADS * HEAD_DIM
assert 2 * CHUNK == HEAD_DIM
RMS_EPS = 1e-6
L2_EPS = 1e-6
HALO = 16
MXU_COLS = 256
FF_BLOCK = MXU_COLS
GATE_LANES = 128
MAX_ROW_TILE = 768
MAX_PREP_GROUP = 3
MAX_GATE_UNROLL = 11
V7X_VMEM_LIMIT_BYTES = 56 * 1024 * 1024

F32 = jnp.float32
BF16 = jnp.bfloat16


def _row_tile(rows_per_seq):
    best = CHUNK
    for t in range(CHUNK, MAX_ROW_TILE + 1, CHUNK):
        if rows_per_seq % t == 0:
            best = t
    return best


def _largest_divisor(n, cap):
    return max(g for g in range(1, cap + 1) if n % g == 0)


def _mm(a, b):
    return jnp.dot(a.astype(BF16), b.astype(BF16), preferred_element_type=F32)


def _mm_nt(a, b):
    return lax.dot_general(a.astype(BF16), b.astype(BF16), (((1,), (1,)), ((), ())),
                           preferred_element_type=F32)


def _split3(x):
    hi = x.astype(BF16)
    r1 = x - hi.astype(F32)
    mid = r1.astype(BF16)
    lo = (r1 - mid.astype(F32)).astype(BF16)
    return hi, mid, lo


def _mask_dot(mask01, x):
    m = mask01.astype(BF16)
    hi, mid, lo = _split3(x)
    d = lambda p: jnp.dot(m, p, preferred_element_type=F32)
    return d(hi) + (d(mid) + d(lo))


def _gate_col(d, h):
    return (h % 2) * HEADS + d * (HEADS // 2) + h // 2


def _silu(x):
    return x * jax.nn.sigmoid(x)


def _softplus(x):
    return jnp.maximum(x, 0.0) + jnp.log1p(jnp.exp(-jnp.abs(x)))


def _inproj_kernel(hprev_ref, h_ref, hnext_ref, nw_ref, wqkv_ref, wz_ref, wsc_ref, wba_ref,
                   cw_ref, scw_ref,
                   q_ref, k_ref, v_ref, z_ref, ysc_ref, ba_ref,
                   hn_scr, pq_scr, ps_scr, *, tm, n_tiles):
    j = pl.program_id(1)
    nw = nw_ref[...]

    def norm(x):
        ms = jnp.mean(x * x, axis=-1, keepdims=True)
        return (x * lax.rsqrt(ms + RMS_EPS) * nw).astype(BF16)

    nxt = jnp.where(j == n_tiles - 1, 0.0, hnext_ref[0])
    hn_scr[0:HALO, :] = norm(hprev_ref[0])
    hn_scr[HALO:HALO + tm, :] = norm(h_ref[0])
    hn_scr[HALO + tm:, :] = norm(nxt)

    qkv_refs = (q_ref, k_ref, v_ref)
    sc_width = scw_ref.shape[1]

    def conv3(win, w):
        return w[1:2] * win[8:8 + CHUNK] + w[0:1] * win[7:7 + CHUNK] + w[2:3] * win[9:9 + CHUNK]

    def window(scr, s, lanes):
        return scr[s + HALO - 8:s + HALO + CHUNK + 8, lanes]

    def qkv_epilogue(g):
        lanes = slice(g * HEAD_DIM, (g + 1) * HEAD_DIM)
        out_lanes = slice((g % HEADS) * HEAD_DIM, (g % HEADS + 1) * HEAD_DIM)
        for s in range(0, tm, CHUNK):
            y = _silu(conv3(window(pq_scr, s, lanes), cw_ref[:, lanes]))
            if g < 2 * HEADS:
                y = y * lax.rsqrt(jnp.sum(y * y, axis=-1, keepdims=True) + L2_EPS)
            if g < HEADS:
                y = y * (HEAD_DIM ** -0.5)
            if s < N_PAD:
                row = j * tm + s + lax.broadcasted_iota(jnp.int32, (CHUNK, 1), 0)
                y = jnp.where(row >= N_PAD, y, 0.0)
            qkv_refs[g // HEADS][0, s:s + CHUNK, out_lanes] = y.astype(BF16)

    def sc_epilogue(g):
        lanes = slice(g * HEAD_DIM, (g + 1) * HEAD_DIM)
        lanes_c = slice(sc_width + g * HEAD_DIM, sc_width + (g + 1) * HEAD_DIM)
        lanes_h = slice(2 * sc_width + g * HEAD_DIM, 2 * sc_width + (g + 1) * HEAD_DIM)
        for s in range(0, tm, CHUNK):
            gate = ps_scr[s + HALO:s + HALO + CHUNK, lanes]
            p = window(ps_scr, s, lanes_c) * window(ps_scr, s, lanes_h)
            ysc_ref[0, s:s + CHUNK, lanes] = (gate * conv3(p, scw_ref[:, lanes])).astype(BF16)

    for blk in range(3 * DN_WIDTH // MXU_COLS):
        cols = slice(blk * MXU_COLS, (blk + 1) * MXU_COLS)
        pq_scr[:, cols] = jnp.dot(hn_scr[...], wqkv_ref[:, cols], preferred_element_type=F32)
        for g in range(blk * MXU_COLS // HEAD_DIM, (blk + 1) * MXU_COLS // HEAD_DIM):
            qkv_epilogue(g)
    for blk in range(sc_width // MXU_COLS):
        for part in range(3):
            cols = slice(part * sc_width + blk * MXU_COLS, part * sc_width + (blk + 1) * MXU_COLS)
            ps_scr[:, cols] = jnp.dot(hn_scr[...], wsc_ref[:, cols], preferred_element_type=F32)
        for g in range(blk * MXU_COLS // HEAD_DIM, (blk + 1) * MXU_COLS // HEAD_DIM):
            sc_epilogue(g)
    hm = hn_scr[HALO:HALO + tm, :]
    z_ref[0] = jnp.dot(hm, wz_ref[...], preferred_element_type=F32).astype(BF16)
    ba_ref[0] = jnp.dot(hm, wba_ref[...], preferred_element_type=F32)


def _inproj(h, nw, wqkv, wz, wsc, wba, cw, scw, *, tm):
    bsz, lp, d = h.shape
    n_tiles = lp // tm
    hb = tm // HALO
    const = lambda shape: pl.BlockSpec(shape, lambda b, j: (0,) * len(shape),
                                       pipeline_mode=pl.Buffered(1))
    row_out = lambda w: pl.BlockSpec((1, tm, w), lambda b, j: (b, j, 0))
    sc_w = wsc.shape[1]
    return pl.pallas_call(
        functools.partial(_inproj_kernel, tm=tm, n_tiles=n_tiles),
        grid=(bsz, n_tiles),
        in_specs=[
            pl.BlockSpec((1, HALO, d), lambda b, j: (b, jnp.maximum(j * hb - 1, 0), 0)),
            pl.BlockSpec((1, tm, d), lambda b, j: (b, j, 0)),
            pl.BlockSpec((1, HALO, d), lambda b, j: (b, jnp.minimum((j + 1) * hb, lp // HALO - 1), 0)),
            const((1, d)),
            const(wqkv.shape), const(wz.shape), const(wsc.shape), const(wba.shape),
            const(cw.shape), const(scw.shape),
        ],
        out_specs=[row_out(DN_WIDTH), row_out(DN_WIDTH), row_out(DN_WIDTH), row_out(DN_WIDTH),
                   row_out(sc_w // 3), row_out(GATE_LANES)],
        out_shape=[jax.ShapeDtypeStruct((bsz, lp, DN_WIDTH), BF16)] * 4
        + [jax.ShapeDtypeStruct((bsz, lp, sc_w // 3), BF16),
           jax.ShapeDtypeStruct((bsz, lp, GATE_LANES), F32)],
        scratch_shapes=[pltpu.VMEM((tm + 2 * HALO, d), BF16),
                        pltpu.VMEM((tm + 2 * HALO, 3 * DN_WIDTH), F32),
                        pltpu.VMEM((tm + 2 * HALO, sc_w), F32)],
        compiler_params=pltpu.CompilerParams(
            dimension_semantics=("arbitrary", "arbitrary"),
            vmem_limit_bytes=V7X_VMEM_LIMIT_BYTES),
        name="inproj",
    )(h, h, h, nw, wqkv, wz, wsc, wba, cw, scw)


def _blockdiag_pair(p, same_block):
    return jnp.where(same_block, jnp.concatenate([p, p], axis=0), 0.0).astype(BF16)


def _unit_tri_inverse_stages(ms, ii, jj, same_block, out):
    mm = lambda a, b: jnp.dot(a.astype(BF16), _blockdiag_pair(b, same_block),
                              preferred_element_type=F32)
    eye = (ii == jj).astype(F32)
    same8 = (ii >> 3) == (jj >> 3)
    n1 = [jnp.where(same8, m, 0.0) for m in ms]
    n2 = [mm(n, n) for n in n1]
    yield
    both = [mm(jnp.concatenate([b, eye - a], axis=0), b) for a, b in zip(n1, n2)]
    n4 = [p[:CHUNK] for p in both]
    y = [(eye - a) + p[CHUNK:] for a, p in zip(n1, both)]
    yield
    x = [a + mm(a, b) for a, b in zip(y, n4)]
    yield
    for ls in (3, 4, 5):
        pair = ((ii >> (ls + 1)) == (jj >> (ls + 1))) & ((ii >> ls) != (jj >> ls))
        t = [mm(a, jnp.where(pair, m, 0.0)) for a, m in zip(x, ms)]
        yield
        x = [a - mm(b, a) for a, b in zip(x, t)]
        yield
    out.extend(x)


def _gdn_kernel(q_ref, k_ref, v_ref, ba_ref, alr_ref, dtr_ref,
                o_ref, beta_scr, gcol_scr, grow_scr, s_scr, u_scr, wq_scr, ka_scr,
                *, n_chunks, group, unroll):
    ii = lax.broadcasted_iota(jnp.int32, (2 * CHUNK, CHUNK), 0)
    jj = lax.broadcasted_iota(jnp.int32, (2 * CHUNK, CHUNK), 1)
    cum01 = jnp.where(ii < CHUNK, (jj <= ii).astype(F32), (jj >= ii - CHUNK).astype(F32))
    lane = lax.broadcasted_iota(jnp.int32, (CHUNK, GATE_LANES), 1)
    fwd_lane = (lane & (HEADS - 1)) < HEADS // 2

    def gates(c, carry):
        r0 = pl.multiple_of(c * CHUNK, CHUNK)
        ba = ba_ref[0, pl.ds(r0, CHUNK), :]
        beta_scr[pl.ds(r0, CHUNK), :] = jax.nn.sigmoid(ba)
        a_shift = pltpu.roll(ba, GATE_LANES - 2 * HEADS, axis=1)
        g = -jnp.exp(alr_ref[...]) * _softplus(a_shift + dtr_ref[...])
        both = _mask_dot(cum01, g)
        g_cum = jnp.where(fwd_lane, both[:CHUNK], both[CHUNK:])
        gcol_scr[pl.ds(r0, CHUNK), :] = g_cum
        g8 = g_cum.T[:2 * HEADS]
        grow_scr[c] = jnp.concatenate([g8[:HEADS], g8[HEADS:]], axis=1)
        return carry

    lax.fori_loop(0, n_chunks, gates, 0, unroll=unroll)

    s_scr[...] = jnp.zeros(s_scr.shape, F32)
    o_ref[...] = jnp.zeros(o_ref.shape, F32)

    last = (CHUNK - 1, 0)
    head_lanes = [slice(h * HEAD_DIM, (h + 1) * HEAD_DIM) for h in range(HEADS)]
    pi = lax.broadcasted_iota(jnp.int32, (CHUNK, 2 * CHUNK), 0)
    pl_lane = lax.broadcasted_iota(jnp.int32, (CHUNK, 2 * CHUNK), 1)
    pj = pl_lane & (CHUNK - 1)
    first_half = pl_lane < CHUNK
    same_block = ((lax.broadcasted_iota(jnp.int32, (2 * CHUNK, 2 * CHUNK), 0) >> 6)
                  == (lax.broadcasted_iota(jnp.int32, (2 * CHUNK, 2 * CHUNK), 1) >> 6))
    incl = (pj <= pi, pj >= pi)
    strict = (pj < pi, pj > pi)
    zeros_k = jnp.zeros((CHUNK, HEAD_DIM), BF16)
    zeros_r = jnp.zeros((CHUNK, 2 * HEAD_DIM), F32)

    n_groups = n_chunks // group
    steps = [(d, t) for d in range(2) for t in range(group)]
    pairs = [(d, t, hp) for d, t in steps for hp in range(HEADS // 2)]

    def chunk_of(gi, d, t):
        i = gi * group + t
        return i if d == 0 else n_chunks - 1 - i

    def chunk_rows(c):
        return pl.ds(pl.multiple_of(c * CHUNK, CHUNK), CHUNK)

    def slot_index(d, t, h):
        return (d * HEADS + h) * group + t

    def prep_stages(gi, slot):
        rows = {(d, t): chunk_rows(chunk_of(gi, d, t)) for d, t in steps}
        q_c, k_c, kk, qk = {}, {}, [], []
        for d, t, hp in pairs:
            ha, hb = 2 * hp, 2 * hp + 1
            for h in (ha, hb):
                q_c[d, t, h] = q_ref[0, rows[d, t], head_lanes[h]]
                k_c[d, t, h] = k_ref[0, rows[d, t], head_lanes[h]]
            ka, kb = k_c[d, t, ha], k_c[d, t, hb]
            k_bd_t = jnp.concatenate([jnp.concatenate([ka, zeros_k], axis=1),
                                      jnp.concatenate([zeros_k, kb], axis=1)], axis=0)
            kq = jnp.concatenate([jnp.concatenate([ka, kb], axis=1),
                                  jnp.concatenate([q_c[d, t, ha], q_c[d, t, hb]], axis=1)], axis=0)
            both = _mm_nt(kq, k_bd_t)
            kk.append(both[:CHUNK])
            qk.append(both[CHUNK:])
        yield
        g_all = {dt: gcol_scr[rows[dt], :] for dt in steps}
        b_all = {dt: beta_scr[rows[dt], :] for dt in steps}
        g_rows = {(d, t): grow_scr[chunk_of(gi, d, t)] for d, t in steps}
        g_c, b_c, m, attn = {}, {}, [], []
        for a, (d, t, hp) in enumerate(pairs):
            for h in (2 * hp, 2 * hp + 1):
                col = _gate_col(d, h)
                g_c[d, t, h] = g_all[d, t][:, col:col + 1]
                b_c[d, t, h] = b_all[d, t][:, col:col + 1]
            g_cp = jnp.where(first_half, g_c[d, t, 2 * hp], g_c[d, t, 2 * hp + 1])
            b_cp = jnp.where(first_half, b_c[d, t, 2 * hp], b_c[d, t, 2 * hp + 1])
            row = d * (HEADS // 2) + hp
            g_rp = g_rows[d, t][row:row + 1, :]
            decay = jnp.exp(jnp.where(incl[d], g_cp - g_rp, -jnp.inf))
            m.append(jnp.where(strict[d], b_cp * kk[a] * decay, 0.0))
            attn.append(qk[a] * decay)
        x = []
        yield from _unit_tri_inverse_stages(m, pi, pj, same_block, x)
        for a, (d, t, hp) in enumerate(pairs):
            rhs = []
            for h in (2 * hp, 2 * hp + 1):
                k_f = k_c[d, t, h].astype(F32)
                v_f = v_ref[0, rows[d, t], head_lanes[h]].astype(F32)
                e_g = jnp.exp(g_c[d, t, h])
                g_last = g_c[d, t, h][last[d]:last[d] + 1, :]
                idx = slot_index(d, t, h)
                wq_scr[slot, idx, CHUNK:, :] = (q_c[d, t, h].astype(F32) * e_g).astype(BF16)
                ka_scr[slot, idx, :HEAD_DIM, :] = (k_f * jnp.exp(g_last - g_c[d, t, h])).T.astype(BF16)
                rhs.append(jnp.concatenate([v_f * b_c[d, t, h], k_f * (b_c[d, t, h] * e_g)], axis=1))
            rhs_bd = jnp.concatenate([jnp.concatenate([rhs[0], zeros_r], axis=1),
                                      jnp.concatenate([zeros_r, rhs[1]], axis=1)], axis=0)
            uw = _mm(x[a], rhs_bd)
            attn_b = attn[a].astype(BF16)
            for n, h in enumerate((2 * hp, 2 * hp + 1)):
                idx = slot_index(d, t, h)
                u_scr[slot, idx] = uw[:, 2 * n * HEAD_DIM:(2 * n + 1) * HEAD_DIM]
                wq_scr[slot, idx, :CHUNK, :] = (
                    uw[:, (2 * n + 1) * HEAD_DIM:(2 * n + 2) * HEAD_DIM].astype(BF16))
                ka_scr[slot, idx, HEAD_DIM:, :] = attn_b[:, n * CHUNK:(n + 1) * CHUNK]
        yield

    def scan_stages(gi, slot):
        for t in range(group):
            step = [(d, h) for d in range(2) for h in range(HEADS)]
            s = [s_scr[d * HEADS + h] for d, h in step]
            wqs = [_mm(wq_scr[slot, slot_index(d, t, h)], s[a]) for a, (d, h) in enumerate(step)]
            yield
            v_new = [u_scr[slot, slot_index(d, t, h)] - wqs[a][:CHUNK]
                     for a, (d, h) in enumerate(step)]
            kav = [_mm(ka_scr[slot, slot_index(d, t, h)], v_new[a]) for a, (d, h) in enumerate(step)]
            for a, (d, h) in enumerate(step):
                c = chunk_of(gi, d, t)
                col = _gate_col(d, h)
                edge = gcol_scr[pl.ds(pl.multiple_of(c * CHUNK + (last[d] // 8) * 8, 8), 8), :]
                g_last = edge[last[d] % 8:last[d] % 8 + 1, col:col + 1]
                s_scr[d * HEADS + h] = s[a] * jnp.exp(g_last) + kav[a][:HEAD_DIM]
                o_ref[0, chunk_rows(c), head_lanes[h]] += wqs[a][CHUNK:] + kav[a][HEAD_DIM:]
            yield

    for _ in prep_stages(0, 0):
        pass

    def body(gi, carry):
        slot = gi % 2
        for _ in itertools.zip_longest(prep_stages(gi + 1, 1 - slot), scan_stages(gi, slot)):
            pass
        return carry

    lax.fori_loop(0, n_groups - 1, body, 0)
    for _ in scan_stages(n_groups - 1, (n_groups - 1) % 2):
        pass


def _gdn(q, k, v, ba, alr, dtr):
    bsz, lp, _ = q.shape
    n_chunks = lp // CHUNK
    seq = lambda w: pl.BlockSpec((1, lp, w), lambda b: (b, 0, 0))
    small = lambda a: pl.BlockSpec(a.shape, lambda b: (0,) * a.ndim)
    group = _largest_divisor(n_chunks, MAX_PREP_GROUP)
    n_slot = 2 * HEADS * group
    return pl.pallas_call(
        functools.partial(_gdn_kernel, n_chunks=n_chunks, group=group,
                          unroll=_largest_divisor(n_chunks, MAX_GATE_UNROLL)),
        grid=(bsz,),
        in_specs=[seq(DN_WIDTH), seq(DN_WIDTH), seq(DN_WIDTH), seq(GATE_LANES),
                  small(alr), small(dtr)],
        out_specs=seq(DN_WIDTH),
        out_shape=jax.ShapeDtypeStruct((bsz, lp, DN_WIDTH), F32),
        scratch_shapes=[pltpu.VMEM((lp, GATE_LANES), F32),
                        pltpu.VMEM((lp, GATE_LANES), F32),
                        pltpu.VMEM((n_chunks, HEADS, 2 * CHUNK), F32),
                        pltpu.VMEM((2 * HEADS, HEAD_DIM, HEAD_DIM), F32),
                        pltpu.VMEM((2, n_slot, CHUNK, HEAD_DIM), F32),
                        pltpu.VMEM((2, n_slot, 2 * CHUNK, HEAD_DIM), BF16),
                        pltpu.VMEM((2, n_slot, HEAD_DIM + CHUNK, CHUNK), BF16)],
        compiler_params=pltpu.CompilerParams(
            dimension_semantics=("arbitrary",),
            vmem_limit_bytes=V7X_VMEM_LIMIT_BYTES),
        name="gdn",
    )(q, k, v, ba, alr, dtr)


def _outffn_kernel(h_ref, o_ref, z_ref, ysc_ref, dnw_ref, wo_ref, wsc_ref, n2_ref,
                   wg_ref, wu_ref, wd_ref, fw_ref, out_ref, og_scr, hn_scr, *, final):
    dnw = dnw_ref[...]
    for hd in range(HEADS):
        lanes = slice(hd * HEAD_DIM, (hd + 1) * HEAD_DIM)
        o = o_ref[:, lanes]
        o = o * lax.rsqrt(jnp.mean(o * o, axis=-1, keepdims=True) + RMS_EPS)
        og_scr[:, lanes] = (o * dnw[:, lanes] * _silu(z_ref[:, lanes].astype(F32))).astype(BF16)

    h1 = (h_ref[...]
          + jnp.dot(og_scr[...], wo_ref[...], preferred_element_type=F32)
          + jnp.dot(ysc_ref[...], wsc_ref[...], preferred_element_type=F32))
    out_ref[...] = h1
    ms = jnp.mean(h1 * h1, axis=-1, keepdims=True)
    hn_scr[...] = (h1 * lax.rsqrt(ms + RMS_EPS) * n2_ref[...]).astype(BF16)

    d_ff = wg_ref.shape[1]
    for f in range(d_ff // FF_BLOCK):
        cols = slice(f * FF_BLOCK, (f + 1) * FF_BLOCK)
        gate = jnp.dot(hn_scr[...], wg_ref[:, cols], preferred_element_type=F32)
        up = jnp.dot(hn_scr[...], wu_ref[:, cols], preferred_element_type=F32)
        act = (_silu(gate) * up).astype(BF16)
        out_ref[...] += jnp.dot(act, wd_ref[cols, :], preferred_element_type=F32)

    if final:
        h2 = out_ref[...]
        ms2 = jnp.mean(h2 * h2, axis=-1, keepdims=True)
        out_ref[...] = h2 * lax.rsqrt(ms2 + RMS_EPS) * fw_ref[...]


def _outffn(h, o, z, ysc, dnw, wo, wsc, n2, wg, wu, wd, fw, *, tm, final):
    rows, d = h.shape
    assert wg.shape[1] % FF_BLOCK == 0
    const = lambda a: pl.BlockSpec(a.shape, lambda i: (0,) * a.ndim, pipeline_mode=pl.Buffered(1))
    row = lambda w: pl.BlockSpec((tm, w), lambda i: (i, 0))
    return pl.pallas_call(
        functools.partial(_outffn_kernel, final=final),
        grid=(rows // tm,),
        in_specs=[row(d), row(DN_WIDTH), row(DN_WIDTH), row(ysc.shape[1]),
                  const(dnw), const(wo), const(wsc), const(n2),
                  const(wg), const(wu), const(wd), const(fw)],
        out_specs=row(d),
        out_shape=jax.ShapeDtypeStruct((rows, d), F32),
        scratch_shapes=[pltpu.VMEM((tm, DN_WIDTH), BF16), pltpu.VMEM((tm, d), BF16)],
        compiler_params=pltpu.CompilerParams(
            dimension_semantics=("arbitrary",),
            vmem_limit_bytes=V7X_VMEM_LIMIT_BYTES),
        name="outffn",
    )(h, o, z, ysc, dnw, wo, wsc, n2, wg, wu, wd, fw)


def kernel(x, meta_tokens, norm1_w, w_in, conv_qkv_w, a_log, dt_bias, dn_norm_w, sc_conv_w, w_out,
           norm2_w, w_gate_up, w_down, final_norm_w):
    bsz, seq, d = x.shape
    depth = w_in.shape[0]
    sc_width = sc_conv_w.shape[-1]
    d_ff = w_down.shape[1]
    assert conv_qkv_w.shape[-1] == 3 * DN_WIDTH and dn_norm_w.shape[-1] == HEAD_DIM
    assert w_in.shape[-1] == 4 * DN_WIDTH + 4 * HEADS + 3 * sc_width
    assert (N_PAD + N_META + seq) % CHUNK == 0
    lp = N_PAD + N_META + seq
    n_chunks = lp // CHUNK
    tm = _row_tile(lp)

    meta = jnp.broadcast_to(meta_tokens.astype(x.dtype)[None], (bsz, N_META, d))
    h = jnp.concatenate([jnp.zeros((bsz, N_PAD, d), x.dtype), meta, x], axis=1)

    gate_perm = [0] * (2 * HEADS)
    for dd in range(2):
        for hh in range(HEADS):
            gate_perm[_gate_col(dd, hh)] = dd * HEADS + hh
    gate_perm = jnp.array(gate_perm, jnp.int32)
    gate0 = 4 * DN_WIDTH
    sc0 = gate0 + 4 * HEADS
    for l in range(depth):
        wqkv = w_in[l, :, :3 * DN_WIDTH].astype(BF16)
        wz = w_in[l, :, 3 * DN_WIDTH:gate0].astype(BF16)
        wba = jnp.concatenate([w_in[l][:, gate0 + gate_perm], w_in[l][:, gate0 + 2 * HEADS + gate_perm]],
                              axis=1)
        wba = jnp.pad(wba, ((0, 0), (0, GATE_LANES - 4 * HEADS))).astype(BF16)
        wsc = w_in[l, :, sc0:].astype(BF16)
        q, k, v, z, ysc, ba = _inproj(h, norm1_w[l][None], wqkv, wz, wsc, wba,
                                      conv_qkv_w[l], sc_conv_w[l], tm=tm)

        pad_row = lambda t: jnp.pad(t.reshape(2 * HEADS)[gate_perm].astype(F32)[None],
                                    ((0, 0), (0, GATE_LANES - 2 * HEADS)))
        o = _gdn(q, k, v, ba, pad_row(a_log[l]), pad_row(dt_bias[l]))

        flat = lambda t: t.reshape(bsz * lp, t.shape[-1])
        h = _outffn(flat(h), flat(o), flat(z), flat(ysc),
                    jnp.tile(dn_norm_w[l], HEADS)[None].astype(F32),
                    w_out[l, :DN_WIDTH].astype(BF16), w_out[l, DN_WIDTH:].astype(BF16),
                    norm2_w[l][None],
                    w_gate_up[l, :, :d_ff].astype(BF16), w_gate_up[l, :, d_ff:].astype(BF16),
                    w_down[l].astype(BF16), final_norm_w[None],
                    tm=tm, final=(l == depth - 1)).reshape(bsz, lp, d)

    return h[:, N_PAD + N_META:]
```

```python
import functools
import itertools

import jax
import jax.numpy as jnp
from jax import lax
from jax.experimental import pallas as pl
from jax.experimental.pallas import tpu as pltpu

N_META = 16
CHUNK = 64
N_PAD = (-N_META) % CHUNK
HEADS = 4
HEAD_DIM = 128
DN_WIDTH = HEADS * HEAD_DIM
assert 2 * CHUNK == HEAD_DIM
RMS_EPS = 1e-6
L2_EPS = 1e-6
HALO = 16
MXU_COLS = 256
FF_BLOCK = MXU_COLS
GATE_LANES = 128
MAX_ROW_TILE = 768
MAX_FINAL_TILE = 1024
MAX_PREP_GROUP = 3
MAX_GATE_UNROLL = 11
V7X_VMEM_LIMIT_BYTES = 56 * 1024 * 1024

F32 = jnp.float32
BF16 = jnp.bfloat16


def _row_tile(rows_per_seq):
    best = CHUNK
    for t in range(CHUNK, MAX_ROW_TILE + 1, CHUNK):
        if rows_per_seq % t == 0:
            best = t
    return best


def _final_tile(seq):
    return max(t for t in range(8, MAX_FINAL_TILE + 1, 8) if seq % t == 0)


def _largest_divisor(n, cap):
    return max(g for g in range(1, cap + 1) if n % g == 0)


def _mm(a, b):
    return jnp.dot(a.astype(BF16), b.astype(BF16), preferred_element_type=F32)


def _mm_nt(a, b):
    return lax.dot_general(a.astype(BF16), b.astype(BF16), (((1,), (1,)), ((), ())),
                           preferred_element_type=F32)


def _split3(x):
    hi = x.astype(BF16)
    r1 = x - hi.astype(F32)
    mid = r1.astype(BF16)
    lo = (r1 - mid.astype(F32)).astype(BF16)
    return hi, mid, lo


def _mask_dot(mask01, x):
    m = mask01.astype(BF16)
    hi, mid, lo = _split3(x)
    d = lambda p: jnp.dot(m, p, preferred_element_type=F32)
    return d(hi) + (d(mid) + d(lo))


def _gate_col(d, h):
    return (h % 2) * HEADS + d * (HEADS // 2) + h // 2


def _silu(x):
    return x * jax.nn.sigmoid(x)


def _softplus(x):
    return jnp.maximum(x, 0.0) + jnp.log1p(jnp.exp(-jnp.abs(x)))


def _inproj_kernel(hprev_ref, h_ref, hnext_ref, nw_ref, wqkv_ref, wz_ref, wsc_ref, wba_ref,
                   cw_ref, scw_ref,
                   q_ref, k_ref, v_ref, z_ref, ysc_ref, ba_ref,
                   hn_scr, pq_scr, ps_scr, *, tm, n_tiles):
    j = pl.program_id(1)
    nw = nw_ref[...]

    def norm(x):
        ms = jnp.mean(x * x, axis=-1, keepdims=True)
        return (x * lax.rsqrt(ms + RMS_EPS) * nw).astype(BF16)

    nxt = jnp.where(j == n_tiles - 1, 0.0, hnext_ref[0])
    hn_scr[0:HALO, :] = norm(hprev_ref[0])
    hn_scr[HALO:HALO + tm, :] = norm(h_ref[0])
    hn_scr[HALO + tm:, :] = norm(nxt)

    qkv_refs = (q_ref, k_ref, v_ref)
    sc_width = scw_ref.shape[1]

    def conv3(win, w):
        return w[1:2] * win[8:8 + CHUNK] + w[0:1] * win[7:7 + CHUNK] + w[2:3] * win[9:9 + CHUNK]

    def window(scr, s, lanes):
        return scr[s + HALO - 8:s + HALO + CHUNK + 8, lanes]

    def qkv_epilogue(g):
        lanes = slice(g * HEAD_DIM, (g + 1) * HEAD_DIM)
        out_lanes = slice((g % HEADS) * HEAD_DIM, (g % HEADS + 1) * HEAD_DIM)
        for s in range(0, tm, CHUNK):
            y = _silu(conv3(window(pq_scr, s, lanes), cw_ref[:, lanes]))
            if g < 2 * HEADS:
                inv = lax.rsqrt(jnp.sum(y * y, axis=-1, keepdims=True) + L2_EPS)
                y = y * (inv * (HEAD_DIM ** -0.5) if g < HEADS else inv)
            if s < N_PAD:
                row = j * tm + s + lax.broadcasted_iota(jnp.int32, (CHUNK, 1), 0)
                y = jnp.where(row >= N_PAD, y, 0.0)
            qkv_refs[g // HEADS][0, s:s + CHUNK, out_lanes] = y.astype(BF16)

    def sc_epilogue(g):
        lanes = slice(g * HEAD_DIM, (g + 1) * HEAD_DIM)
        lanes_c = slice(sc_width + g * HEAD_DIM, sc_width + (g + 1) * HEAD_DIM)
        lanes_h = slice(2 * sc_width + g * HEAD_DIM, 2 * sc_width + (g + 1) * HEAD_DIM)
        for s in range(0, tm, CHUNK):
            gate = ps_scr[s + HALO:s + HALO + CHUNK, lanes]
            p = window(ps_scr, s, lanes_c) * window(ps_scr, s, lanes_h)
            ysc_ref[0, s:s + CHUNK, lanes] = (gate * conv3(p, scw_ref[:, lanes])).astype(BF16)

    for blk in range(3 * DN_WIDTH // MXU_COLS):
        cols = slice(blk * MXU_COLS, (blk + 1) * MXU_COLS)
        pq_scr[:, cols] = jnp.dot(hn_scr[...], wqkv_ref[:, cols], preferred_element_type=F32)
        for g in range(blk * MXU_COLS // HEAD_DIM, (blk + 1) * MXU_COLS // HEAD_DIM):
            qkv_epilogue(g)
    for blk in range(sc_width // MXU_COLS):
        for part in range(3):
            cols = slice(part * sc_width + blk * MXU_COLS, part * sc_width + (blk + 1) * MXU_COLS)
            ps_scr[:, cols] = jnp.dot(hn_scr[...], wsc_ref[:, cols], preferred_element_type=F32)
        for g in range(blk * MXU_COLS // HEAD_DIM, (blk + 1) * MXU_COLS // HEAD_DIM):
            sc_epilogue(g)
    hm = hn_scr[HALO:HALO + tm, :]
    z_ref[0] = jnp.dot(hm, wz_ref[...], preferred_element_type=F32).astype(BF16)
    ba_ref[0] = jnp.dot(hm, wba_ref[...], preferred_element_type=F32)


def _inproj(h, nw, wqkv, wz, wsc, wba, cw, scw, *, tm):
    bsz, lp, d = h.shape
    n_tiles = lp // tm
    hb = tm // HALO
    const = lambda shape: pl.BlockSpec(shape, lambda b, j: (0,) * len(shape),
                                       pipeline_mode=pl.Buffered(1))
    row_out = lambda w: pl.BlockSpec((1, tm, w), lambda b, j: (b, j, 0))
    sc_w = wsc.shape[1]
    return pl.pallas_call(
        functools.partial(_inproj_kernel, tm=tm, n_tiles=n_tiles),
        grid=(bsz, n_tiles),
        in_specs=[
            pl.BlockSpec((1, HALO, d), lambda b, j: (b, jnp.maximum(j * hb - 1, 0), 0)),
            pl.BlockSpec((1, tm, d), lambda b, j: (b, j, 0)),
            pl.BlockSpec((1, HALO, d), lambda b, j: (b, jnp.minimum((j + 1) * hb, lp // HALO - 1), 0)),
            const((1, d)),
            const(wqkv.shape), const(wz.shape), const(wsc.shape), const(wba.shape),
            const(cw.shape), const(scw.shape),
        ],
        out_specs=[row_out(DN_WIDTH), row_out(DN_WIDTH), row_out(DN_WIDTH), row_out(DN_WIDTH),
                   row_out(sc_w // 3), row_out(GATE_LANES)],
        out_shape=[jax.ShapeDtypeStruct((bsz, lp, DN_WIDTH), BF16)] * 4
        + [jax.ShapeDtypeStruct((bsz, lp, sc_w // 3), BF16),
           jax.ShapeDtypeStruct((bsz, lp, GATE_LANES), F32)],
        scratch_shapes=[pltpu.VMEM((tm + 2 * HALO, d), BF16),
                        pltpu.VMEM((tm + 2 * HALO, 3 * DN_WIDTH), F32),
                        pltpu.VMEM((tm + 2 * HALO, sc_w), F32)],
        compiler_params=pltpu.CompilerParams(
            dimension_semantics=("arbitrary", "arbitrary"),
            vmem_limit_bytes=V7X_VMEM_LIMIT_BYTES),
        name="inproj",
    )(h, h, h, nw, wqkv, wz, wsc, wba, cw, scw)


def _blockdiag_pair(p, same_block):
    return jnp.where(same_block, jnp.concatenate([p, p], axis=0), 0.0).astype(BF16)


def _unit_tri_inverse_stages(ms, ii, jj, same_block, out):
    mm = lambda a, b: jnp.dot(a.astype(BF16), _blockdiag_pair(b, same_block),
                              preferred_element_type=F32)
    eye = (ii == jj).astype(F32)
    same8 = (ii >> 3) == (jj >> 3)
    n1 = [jnp.where(same8, m, 0.0) for m in ms]
    n2 = [mm(n, n) for n in n1]
    yield
    both = [mm(jnp.concatenate([b, eye - a], axis=0), b) for a, b in zip(n1, n2)]
    n4 = [p[:CHUNK] for p in both]
    y = [(eye - a) + p[CHUNK:] for a, p in zip(n1, both)]
    yield
    x = [a + mm(a, b) for a, b in zip(y, n4)]
    yield
    for ls in (3, 4, 5):
        pair = ((ii >> (ls + 1)) == (jj >> (ls + 1))) & ((ii >> ls) != (jj >> ls))
        t = [mm(a, jnp.where(pair, m, 0.0)) for a, m in zip(x, ms)]
        yield
        x = [a - mm(b, a) for a, b in zip(x, t)]
        yield
    out.extend(x)


def _gdn_kernel(q_ref, k_ref, v_ref, ba_ref, alr_ref, dtr_ref,
                o_ref, beta_scr, gcol_scr, grow_scr, s_scr, u_scr, wq_scr, ka_scr,
                *, n_chunks, group, unroll):
    ii = lax.broadcasted_iota(jnp.int32, (2 * CHUNK, CHUNK), 0)
    jj = lax.broadcasted_iota(jnp.int32, (2 * CHUNK, CHUNK), 1)
    cum01 = jnp.where(ii < CHUNK, (jj <= ii).astype(F32), (jj >= ii - CHUNK).astype(F32))
    lane = lax.broadcasted_iota(jnp.int32, (CHUNK, GATE_LANES), 1)
    fwd_lane = (lane & (HEADS - 1)) < HEADS // 2

    def gates(c, carry):
        r0 = pl.multiple_of(c * CHUNK, CHUNK)
        ba = ba_ref[0, pl.ds(r0, CHUNK), :]
        beta_scr[pl.ds(r0, CHUNK), :] = jax.nn.sigmoid(ba)
        a_shift = pltpu.roll(ba, GATE_LANES - 2 * HEADS, axis=1)
        g = -jnp.exp(alr_ref[...]) * _softplus(a_shift + dtr_ref[...])
        both = _mask_dot(cum01, g)
        g_cum = jnp.where(fwd_lane, both[:CHUNK], both[CHUNK:])
        gcol_scr[pl.ds(r0, CHUNK), :] = g_cum
        g8 = g_cum.T[:2 * HEADS]
        grow_scr[c] = jnp.concatenate([g8[:HEADS], g8[HEADS:]], axis=1)
        return carry

    lax.fori_loop(0, n_chunks, gates, 0, unroll=unroll)

    s_scr[...] = jnp.zeros(s_scr.shape, F32)
    o_ref[...] = jnp.zeros(o_ref.shape, F32)

    last = (CHUNK - 1, 0)
    head_lanes = [slice(h * HEAD_DIM, (h + 1) * HEAD_DIM) for h in range(HEADS)]
    pi = lax.broadcasted_iota(jnp.int32, (CHUNK, 2 * CHUNK), 0)
    pl_lane = lax.broadcasted_iota(jnp.int32, (CHUNK, 2 * CHUNK), 1)
    pj = pl_lane & (CHUNK - 1)
    first_half = pl_lane < CHUNK
    same_block = ((lax.broadcasted_iota(jnp.int32, (2 * CHUNK, 2 * CHUNK), 0) >> 6)
                  == (lax.broadcasted_iota(jnp.int32, (2 * CHUNK, 2 * CHUNK), 1) >> 6))
    incl = (pj <= pi, pj >= pi)
    strict = (pj < pi, pj > pi)
    zeros_k = jnp.zeros((CHUNK, HEAD_DIM), BF16)
    zeros_r = jnp.zeros((CHUNK, 2 * HEAD_DIM), F32)

    n_groups = n_chunks // group
    steps = [(d, t) for d in range(2) for t in range(group)]
    pairs = [(d, t, hp) for d, t in steps for hp in range(HEADS // 2)]

    def chunk_of(gi, d, t):
        i = gi * group + t
        return i if d == 0 else n_chunks - 1 - i

    def chunk_rows(c):
        return pl.ds(pl.multiple_of(c * CHUNK, CHUNK), CHUNK)

    def slot_index(d, t, h):
        return (d * HEADS + h) * group + t

    def prep_stages(gi, slot):
        rows = {(d, t): chunk_rows(chunk_of(gi, d, t)) for d, t in steps}
        q_c, k_c, kk, qk = {}, {}, [], []
        for d, t, hp in pairs:
            ha, hb = 2 * hp, 2 * hp + 1
            for h in (ha, hb):
                q_c[d, t, h] = q_ref[0, rows[d, t], head_lanes[h]]
                k_c[d, t, h] = k_ref[0, rows[d, t], head_lanes[h]]
            ka, kb = k_c[d, t, ha], k_c[d, t, hb]
            k_bd_t = jnp.concatenate([jnp.concatenate([ka, zeros_k], axis=1),
                                      jnp.concatenate([zeros_k, kb], axis=1)], axis=0)
            kq = jnp.concatenate([jnp.concatenate([ka, kb], axis=1),
                                  jnp.concatenate([q_c[d, t, ha], q_c[d, t, hb]], axis=1)], axis=0)
            both = _mm_nt(kq, k_bd_t)
            kk.append(both[:CHUNK])
            qk.append(both[CHUNK:])
        yield
        g_all = {dt: gcol_scr[rows[dt], :] for dt in steps}
        b_all = {dt: beta_scr[rows[dt], :] for dt in steps}
        g_rows = {(d, t): grow_scr[chunk_of(gi, d, t)] for d, t in steps}
        g_c, b_c, m, attn = {}, {}, [], []
        for a, (d, t, hp) in enumerate(pairs):
            for h in (2 * hp, 2 * hp + 1):
                col = _gate_col(d, h)
                g_c[d, t, h] = g_all[d, t][:, col:col + 1]
                b_c[d, t, h] = b_all[d, t][:, col:col + 1]
            g_cp = jnp.where(first_half, g_c[d, t, 2 * hp], g_c[d, t, 2 * hp + 1])
            b_cp = jnp.where(first_half, b_c[d, t, 2 * hp], b_c[d, t, 2 * hp + 1])
            row = d * (HEADS // 2) + hp
            g_rp = g_rows[d, t][row:row + 1, :]
            decay = jnp.exp(jnp.where(incl[d], g_cp - g_rp, -jnp.inf))
            m.append(jnp.where(strict[d], b_cp * kk[a] * decay, 0.0))
            attn.append(qk[a] * decay)
        x = []
        yield from _unit_tri_inverse_stages(m, pi, pj, same_block, x)
        for a, (d, t, hp) in enumerate(pairs):
            rhs = []
            for h in (2 * hp, 2 * hp + 1):
                k_f = k_c[d, t, h].astype(F32)
                v_f = v_ref[0, rows[d, t], head_lanes[h]].astype(F32)
                e_g = jnp.exp(g_c[d, t, h])
                g_last = g_c[d, t, h][last[d]:last[d] + 1, :]
                idx = slot_index(d, t, h)
                wq_scr[slot, idx, CHUNK:, :] = (q_c[d, t, h].astype(F32) * e_g).astype(BF16)
                ka_scr[slot, idx, :HEAD_DIM, :] = (k_f * jnp.exp(g_last - g_c[d, t, h])).T.astype(BF16)
                rhs.append(jnp.concatenate([v_f * b_c[d, t, h], k_f * (b_c[d, t, h] * e_g)], axis=1))
            rhs_bd = jnp.concatenate([jnp.concatenate([rhs[0], zeros_r], axis=1),
                                      jnp.concatenate([zeros_r, rhs[1]], axis=1)], axis=0)
            uw = _mm(x[a], rhs_bd)
            attn_b = attn[a].astype(BF16)
            for n, h in enumerate((2 * hp, 2 * hp + 1)):
                idx = slot_index(d, t, h)
                u_scr[slot, idx] = uw[:, 2 * n * HEAD_DIM:(2 * n + 1) * HEAD_DIM]
                wq_scr[slot, idx, :CHUNK, :] = (
                    uw[:, (2 * n + 1) * HEAD_DIM:(2 * n + 2) * HEAD_DIM].astype(BF16))
                ka_scr[slot, idx, HEAD_DIM:, :] = attn_b[:, n * CHUNK:(n + 1) * CHUNK]
        yield

    def scan_stages(gi, slot):
        for t in range(group):
            step = [(d, h) for d in range(2) for h in range(HEADS)]
            s = [s_scr[d * HEADS + h] for d, h in step]
            wqs = [_mm(wq_scr[slot, slot_index(d, t, h)], s[a]) for a, (d, h) in enumerate(step)]
            yield
            v_new = [u_scr[slot, slot_index(d, t, h)] - wqs[a][:CHUNK]
                     for a, (d, h) in enumerate(step)]
            kav = [_mm(ka_scr[slot, slot_index(d, t, h)], v_new[a]) for a, (d, h) in enumerate(step)]
            for a, (d, h) in enumerate(step):
                c = chunk_of(gi, d, t)
                col = _gate_col(d, h)
                edge = gcol_scr[pl.ds(pl.multiple_of(c * CHUNK + (last[d] // 8) * 8, 8), 8), :]
                g_last = edge[last[d] % 8:last[d] % 8 + 1, col:col + 1]
                s_scr[d * HEADS + h] = s[a] * jnp.exp(g_last) + kav[a][:HEAD_DIM]
                o_ref[0, chunk_rows(c), head_lanes[h]] += wqs[a][CHUNK:] + kav[a][HEAD_DIM:]
            yield

    for _ in prep_stages(0, 0):
        pass

    def body(gi, carry):
        slot = gi % 2
        for _ in itertools.zip_longest(prep_stages(gi + 1, 1 - slot), scan_stages(gi, slot)):
            pass
        return carry

    lax.fori_loop(0, n_groups - 1, body, 0)
    for _ in scan_stages(n_groups - 1, (n_groups - 1) % 2):
        pass


def _gdn(q, k, v, ba, alr, dtr):
    bsz, lp, _ = q.shape
    n_chunks = lp // CHUNK
    seq = lambda w: pl.BlockSpec((1, lp, w), lambda b: (b, 0, 0))
    small = lambda a: pl.BlockSpec(a.shape, lambda b: (0,) * a.ndim)
    group = _largest_divisor(n_chunks, MAX_PREP_GROUP)
    n_slot = 2 * HEADS * group
    return pl.pallas_call(
        functools.partial(_gdn_kernel, n_chunks=n_chunks, group=group,
                          unroll=_largest_divisor(n_chunks, MAX_GATE_UNROLL)),
        grid=(bsz,),
        in_specs=[seq(DN_WIDTH), seq(DN_WIDTH), seq(DN_WIDTH), seq(GATE_LANES),
                  small(alr), small(dtr)],
        out_specs=seq(DN_WIDTH),
        out_shape=jax.ShapeDtypeStruct((bsz, lp, DN_WIDTH), F32),
        scratch_shapes=[pltpu.VMEM((lp, GATE_LANES), F32),
                        pltpu.VMEM((lp, GATE_LANES), F32),
                        pltpu.VMEM((n_chunks, HEADS, 2 * CHUNK), F32),
                        pltpu.VMEM((2 * HEADS, HEAD_DIM, HEAD_DIM), F32),
                        pltpu.VMEM((2, n_slot, CHUNK, HEAD_DIM), F32),
                        pltpu.VMEM((2, n_slot, 2 * CHUNK, HEAD_DIM), BF16),
                        pltpu.VMEM((2, n_slot, HEAD_DIM + CHUNK, CHUNK), BF16)],
        compiler_params=pltpu.CompilerParams(
            dimension_semantics=("arbitrary",),
            vmem_limit_bytes=V7X_VMEM_LIMIT_BYTES),
        name="gdn",
    )(q, k, v, ba, alr, dtr)


def _outffn_kernel(h_ref, o_ref, z_ref, ysc_ref, dnw_ref, wo_ref, wsc_ref, n2_ref,
                   wg_ref, wu_ref, wd_ref, fw_ref, out_ref, og_scr, hn_scr, *, final):
    if final:
        h_ref, o_ref, z_ref, ysc_ref = (r.at[0] for r in (h_ref, o_ref, z_ref, ysc_ref))
    dnw = dnw_ref[...]
    for hd in range(HEADS):
        lanes = slice(hd * HEAD_DIM, (hd + 1) * HEAD_DIM)
        o = o_ref[:, lanes]
        o = o * lax.rsqrt(jnp.mean(o * o, axis=-1, keepdims=True) + RMS_EPS)
        og_scr[:, lanes] = (o * dnw[:, lanes] * _silu(z_ref[:, lanes].astype(F32))).astype(BF16)

    h1 = (h_ref[...]
          + jnp.dot(og_scr[...], wo_ref[...], preferred_element_type=F32)
          + jnp.dot(ysc_ref[...], wsc_ref[...], preferred_element_type=F32))
    out_ref[...] = h1
    ms = jnp.mean(h1 * h1, axis=-1, keepdims=True)
    hn_scr[...] = (h1 * lax.rsqrt(ms + RMS_EPS) * n2_ref[...]).astype(BF16)

    d_ff = wg_ref.shape[1]
    for f in range(d_ff // FF_BLOCK):
        cols = slice(f * FF_BLOCK, (f + 1) * FF_BLOCK)
        gate = jnp.dot(hn_scr[...], wg_ref[:, cols], preferred_element_type=F32)
        up = jnp.dot(hn_scr[...], wu_ref[:, cols], preferred_element_type=F32)
        act = (_silu(gate) * up).astype(BF16)
        out_ref[...] += jnp.dot(act, wd_ref[cols, :], preferred_element_type=F32)

    if final:
        h2 = out_ref[...]
        ms2 = jnp.mean(h2 * h2, axis=-1, keepdims=True)
        out_ref[...] = h2 * lax.rsqrt(ms2 + RMS_EPS) * fw_ref[...]


def _outffn(h, o, z, ysc, dnw, wo, wsc, n2, wg, wu, wd, fw, *, tm, skip_rows):
    bsz, lp, d = h.shape
    assert wg.shape[1] % FF_BLOCK == 0 and (lp - skip_rows) % tm == 0
    const = lambda a: pl.BlockSpec(a.shape, lambda b, j: (0,) * a.ndim, pipeline_mode=pl.Buffered(1))
    if skip_rows:
        row_in = lambda w: pl.BlockSpec((pl.Element(1), pl.Element(tm), pl.Element(w)),
                                        lambda b, j: (b, pl.multiple_of(skip_rows + j * tm, 8), 0))
    else:
        row_in = lambda w: pl.BlockSpec((None, tm, w), lambda b, j: (b, j, 0))
    return pl.pallas_call(
        functools.partial(_outffn_kernel, final=bool(skip_rows)),
        grid=(bsz, (lp - skip_rows) // tm),
        in_specs=[row_in(d), row_in(DN_WIDTH), row_in(DN_WIDTH), row_in(ysc.shape[-1]),
                  const(dnw), const(wo), const(wsc), const(n2),
                  const(wg), const(wu), const(wd), const(fw)],
        out_specs=pl.BlockSpec((None, tm, d), lambda b, j: (b, j, 0)),
        out_shape=jax.ShapeDtypeStruct((bsz, lp - skip_rows, d), F32),
        scratch_shapes=[pltpu.VMEM((tm, DN_WIDTH), BF16), pltpu.VMEM((tm, d), BF16)],
        compiler_params=pltpu.CompilerParams(
            dimension_semantics=("arbitrary", "arbitrary"),
            vmem_limit_bytes=V7X_VMEM_LIMIT_BYTES),
        name="outffn",
    )(h, o, z, ysc, dnw, wo, wsc, n2, wg, wu, wd, fw)


def kernel(x, meta_tokens, norm1_w, w_in, conv_qkv_w, a_log, dt_bias, dn_norm_w, sc_conv_w, w_out,
           norm2_w, w_gate_up, w_down, final_norm_w):
    bsz, seq, d = x.shape
    depth = w_in.shape[0]
    sc_width = sc_conv_w.shape[-1]
    d_ff = w_down.shape[1]
    assert conv_qkv_w.shape[-1] == 3 * DN_WIDTH and dn_norm_w.shape[-1] == HEAD_DIM
    assert w_in.shape[-1] == 4 * DN_WIDTH + 4 * HEADS + 3 * sc_width
    assert (N_PAD + N_META + seq) % CHUNK == 0
    lp = N_PAD + N_META + seq
    n_chunks = lp // CHUNK
    tm = _row_tile(lp)

    meta = jnp.broadcast_to(meta_tokens.astype(x.dtype)[None], (bsz, N_META, d))
    h = jnp.concatenate([jnp.zeros((bsz, N_PAD, d), x.dtype), meta, x], axis=1)

    gate_perm = [0] * (2 * HEADS)
    for dd in range(2):
        for hh in range(HEADS):
            gate_perm[_gate_col(dd, hh)] = dd * HEADS + hh
    gate_perm = jnp.array(gate_perm, jnp.int32)
    gate0 = 4 * DN_WIDTH
    sc0 = gate0 + 4 * HEADS
    for l in range(depth):
        wqkv = w_in[l, :, :3 * DN_WIDTH].astype(BF16)
        wz = w_in[l, :, 3 * DN_WIDTH:gate0].astype(BF16)
        wba = jnp.concatenate([w_in[l][:, gate0 + gate_perm], w_in[l][:, gate0 + 2 * HEADS + gate_perm]],
                              axis=1)
        wba = jnp.pad(wba, ((0, 0), (0, GATE_LANES - 4 * HEADS))).astype(BF16)
        wsc = w_in[l, :, sc0:].astype(BF16)
        q, k, v, z, ysc, ba = _inproj(h, norm1_w[l][None], wqkv, wz, wsc, wba,
                                      conv_qkv_w[l], sc_conv_w[l], tm=tm)

        pad_row = lambda t: jnp.pad(t.reshape(2 * HEADS)[gate_perm].astype(F32)[None],
                                    ((0, 0), (0, GATE_LANES - 2 * HEADS)))
        o = _gdn(q, k, v, ba, pad_row(a_log[l]), pad_row(dt_bias[l]))

        last = l == depth - 1
        h = _outffn(h, o, z, ysc,
                    jnp.tile(dn_norm_w[l], HEADS)[None].astype(F32),
                    w_out[l, :DN_WIDTH].astype(BF16), w_out[l, DN_WIDTH:].astype(BF16),
                    norm2_w[l][None],
                    w_gate_up[l, :, :d_ff].astype(BF16), w_gate_up[l, :, d_ff:].astype(BF16),
                    w_down[l].astype(BF16), final_norm_w[None],
                    tm=_final_tile(seq) if last else tm,
                    skip_rows=N_PAD + N_META if last else 0)

    return h
```

```python
import functools
import itertools

import jax
import jax.numpy as jnp
from jax import lax
from jax.experimental import pallas as pl
from jax.experimental.pallas import tpu as pltpu

N_META = 16
CHUNK = 64
N_PAD = (-N_META) % CHUNK
HEADS = 4
HEAD_DIM = 128
DN_WIDTH = HEADS * HEAD_DIM
assert 2 * CHUNK == HEAD_DIM
RMS_EPS = 1e-6
L2_EPS = 1e-6
HALO = 16
MXU_COLS = 256
FF_BLOCK = MXU_COLS
GATE_LANES = 128
MAX_ROW_TILE = 768
MAX_FINAL_TILE = 1024
MAX_PREP_GROUP = 3
MAX_GATE_UNROLL = 11
V7X_VMEM_LIMIT_BYTES = 56 * 1024 * 1024

F32 = jnp.float32
BF16 = jnp.bfloat16


def _row_tile(rows_per_seq):
    best = CHUNK
    for t in range(CHUNK, MAX_ROW_TILE + 1, CHUNK):
        if rows_per_seq % t == 0:
            best = t
    return best


def _final_tile(seq):
    return max(t for t in range(8, MAX_FINAL_TILE + 1, 8) if seq % t == 0)


def _largest_divisor(n, cap):
    return max(g for g in range(1, cap + 1) if n % g == 0)


def _mm(a, b):
    return jnp.dot(a.astype(BF16), b.astype(BF16), preferred_element_type=F32)


def _mm_nt(a, b):
    return lax.dot_general(a.astype(BF16), b.astype(BF16), (((1,), (1,)), ((), ())),
                           preferred_element_type=F32)


def _split3(x):
    hi = x.astype(BF16)
    r1 = x - hi.astype(F32)
    mid = r1.astype(BF16)
    lo = (r1 - mid.astype(F32)).astype(BF16)
    return hi, mid, lo


def _mask_dot(mask01, x):
    m = mask01.astype(BF16)
    hi, mid, lo = _split3(x)
    d = lambda p: jnp.dot(m, p, preferred_element_type=F32)
    return d(hi) + (d(mid) + d(lo))


def _gate_col(d, h):
    return (h % 2) * HEADS + d * (HEADS // 2) + h // 2


def _silu(x):
    return x * jax.nn.sigmoid(x)


def _softplus(x):
    return jnp.maximum(x, 0.0) + jnp.log1p(jnp.exp(-jnp.abs(x)))


def _inproj_kernel(hprev_ref, h_ref, hnext_ref, nw_ref, wqkv_ref, wz_ref, wsc_ref, wba_ref,
                   cw_ref, scw_ref,
                   q_ref, k_ref, v_ref, z_ref, ysc_ref, ba_ref,
                   hn_scr, pq_scr, ps_scr, *, tm, n_tiles):
    j = pl.program_id(1)
    nw = nw_ref[...]

    def norm(x):
        ms = jnp.mean(x * x, axis=-1, keepdims=True)
        return (x * lax.rsqrt(ms + RMS_EPS) * nw).astype(BF16)

    nxt = jnp.where(j == n_tiles - 1, 0.0, hnext_ref[0])
    hn_scr[0:HALO, :] = norm(hprev_ref[0])
    hn_scr[HALO:HALO + tm, :] = norm(h_ref[0])
    hn_scr[HALO + tm:, :] = norm(nxt)

    qkv_refs = (q_ref, k_ref, v_ref)
    sc_width = scw_ref.shape[1]

    def conv3(win, w):
        return w[1:2] * win[8:8 + CHUNK] + w[0:1] * win[7:7 + CHUNK] + w[2:3] * win[9:9 + CHUNK]

    def window(scr, s, lanes):
        return scr[s + HALO - 8:s + HALO + CHUNK + 8, lanes]

    def qkv_epilogue(g):
        lanes = slice(g * HEAD_DIM, (g + 1) * HEAD_DIM)
        out_lanes = slice((g % HEADS) * HEAD_DIM, (g % HEADS + 1) * HEAD_DIM)
        for s in range(0, tm, CHUNK):
            y = _silu(conv3(window(pq_scr, s, lanes), cw_ref[:, lanes]))
            if g < 2 * HEADS:
                inv = lax.rsqrt(jnp.sum(y * y, axis=-1, keepdims=True) + L2_EPS)
                y = y * (inv * (HEAD_DIM ** -0.5) if g < HEADS else inv)
            if s < N_PAD:
                row = j * tm + s + lax.broadcasted_iota(jnp.int32, (CHUNK, 1), 0)
                y = jnp.where(row >= N_PAD, y, 0.0)
            qkv_refs[g // HEADS][0, s:s + CHUNK, out_lanes] = y.astype(BF16)

    def sc_epilogue(g):
        lanes = slice(g * HEAD_DIM, (g + 1) * HEAD_DIM)
        lanes_c = slice(sc_width + g * HEAD_DIM, sc_width + (g + 1) * HEAD_DIM)
        lanes_h = slice(2 * sc_width + g * HEAD_DIM, 2 * sc_width + (g + 1) * HEAD_DIM)
        for s in range(0, tm, CHUNK):
            gate = ps_scr[s + HALO:s + HALO + CHUNK, lanes]
            p = window(ps_scr, s, lanes_c) * window(ps_scr, s, lanes_h)
            ysc_ref[0, s:s + CHUNK, lanes] = (gate * conv3(p, scw_ref[:, lanes])).astype(BF16)

    for blk in range(3 * DN_WIDTH // MXU_COLS):
        cols = slice(blk * MXU_COLS, (blk + 1) * MXU_COLS)
        pq_scr[:, cols] = jnp.dot(hn_scr[...], wqkv_ref[:, cols], preferred_element_type=F32)
        for g in range(blk * MXU_COLS // HEAD_DIM, (blk + 1) * MXU_COLS // HEAD_DIM):
            qkv_epilogue(g)
    for blk in range(sc_width // MXU_COLS):
        for part in range(3):
            cols = slice(part * sc_width + blk * MXU_COLS, part * sc_width + (blk + 1) * MXU_COLS)
            ps_scr[:, cols] = jnp.dot(hn_scr[...], wsc_ref[:, cols], preferred_element_type=F32)
        for g in range(blk * MXU_COLS // HEAD_DIM, (blk + 1) * MXU_COLS // HEAD_DIM):
            sc_epilogue(g)
    hm = hn_scr[HALO:HALO + tm, :]
    z_ref[0] = jnp.dot(hm, wz_ref[...], preferred_element_type=F32).astype(BF16)
    ba_ref[0] = jnp.dot(hm, wba_ref[...], preferred_element_type=F32)


def _inproj(h, nw, wqkv, wz, wsc, wba, cw, scw, *, tm):
    bsz, lp, d = h.shape
    n_tiles = lp // tm
    hb = tm // HALO
    const = lambda shape: pl.BlockSpec(shape, lambda b, j: (0,) * len(shape),
                                       pipeline_mode=pl.Buffered(1))
    row_out = lambda w: pl.BlockSpec((1, tm, w), lambda b, j: (b, j, 0))
    sc_w = wsc.shape[1]
    return pl.pallas_call(
        functools.partial(_inproj_kernel, tm=tm, n_tiles=n_tiles),
        grid=(bsz, n_tiles),
        in_specs=[
            pl.BlockSpec((1, HALO, d), lambda b, j: (b, jnp.maximum(j * hb - 1, 0), 0)),
            pl.BlockSpec((1, tm, d), lambda b, j: (b, j, 0)),
            pl.BlockSpec((1, HALO, d), lambda b, j: (b, jnp.minimum((j + 1) * hb, lp // HALO - 1), 0)),
            const((1, d)),
            const(wqkv.shape), const(wz.shape), const(wsc.shape), const(wba.shape),
            const(cw.shape), const(scw.shape),
        ],
        out_specs=[row_out(DN_WIDTH), row_out(DN_WIDTH), row_out(DN_WIDTH), row_out(DN_WIDTH),
                   row_out(sc_w // 3), row_out(GATE_LANES)],
        out_shape=[jax.ShapeDtypeStruct((bsz, lp, DN_WIDTH), BF16)] * 4
        + [jax.ShapeDtypeStruct((bsz, lp, sc_w // 3), BF16),
           jax.ShapeDtypeStruct((bsz, lp, GATE_LANES), F32)],
        scratch_shapes=[pltpu.VMEM((tm + 2 * HALO, d), BF16),
                        pltpu.VMEM((tm + 2 * HALO, 3 * DN_WIDTH), F32),
                        pltpu.VMEM((tm + 2 * HALO, sc_w), F32)],
        compiler_params=pltpu.CompilerParams(
            dimension_semantics=("arbitrary", "arbitrary"),
            allow_input_fusion=[False] * 4 + [True] * 4 + [False] * 2,
            vmem_limit_bytes=V7X_VMEM_LIMIT_BYTES),
        name="inproj",
    )(h, h, h, nw, wqkv, wz, wsc, wba, cw, scw)


def _blockdiag_pair(p, same_block):
    return jnp.where(same_block, jnp.concatenate([p, p], axis=0), 0.0).astype(BF16)


def _unit_tri_inverse_stages(ms, ii, jj, same_block, out):
    mm = lambda a, b: jnp.dot(a.astype(BF16), _blockdiag_pair(b, same_block),
                              preferred_element_type=F32)
    eye = (ii == jj).astype(F32)
    same8 = (ii >> 3) == (jj >> 3)
    n1 = [jnp.where(same8, m, 0.0) for m in ms]
    n2 = [mm(n, n) for n in n1]
    yield
    both = [mm(jnp.concatenate([b, eye - a], axis=0), b) for a, b in zip(n1, n2)]
    n4 = [p[:CHUNK] for p in both]
    y = [(eye - a) + p[CHUNK:] for a, p in zip(n1, both)]
    yield
    x = [a + mm(a, b) for a, b in zip(y, n4)]
    yield
    for ls in (3, 4, 5):
        pair = ((ii >> (ls + 1)) == (jj >> (ls + 1))) & ((ii >> ls) != (jj >> ls))
        t = [mm(a, jnp.where(pair, m, 0.0)) for a, m in zip(x, ms)]
        yield
        x = [a - mm(b, a) for a, b in zip(x, t)]
        yield
    out.extend(x)


def _gdn_kernel(q_ref, k_ref, v_ref, ba_ref, alr_ref, dtr_ref,
                o_ref, beta_scr, gcol_scr, grow_scr, s_scr, u_scr, wq_scr, ka_scr,
                *, n_chunks, group, unroll):
    ii = lax.broadcasted_iota(jnp.int32, (2 * CHUNK, CHUNK), 0)
    jj = lax.broadcasted_iota(jnp.int32, (2 * CHUNK, CHUNK), 1)
    cum01 = jnp.where(ii < CHUNK, (jj <= ii).astype(F32), (jj >= ii - CHUNK).astype(F32))
    lane = lax.broadcasted_iota(jnp.int32, (CHUNK, GATE_LANES), 1)
    fwd_lane = (lane & (HEADS - 1)) < HEADS // 2

    def gates(c, carry):
        r0 = pl.multiple_of(c * CHUNK, CHUNK)
        ba = ba_ref[0, pl.ds(r0, CHUNK), :]
        beta_scr[pl.ds(r0, CHUNK), :] = jax.nn.sigmoid(ba)
        a_shift = pltpu.roll(ba, GATE_LANES - 2 * HEADS, axis=1)
        g = -jnp.exp(alr_ref[...]) * _softplus(a_shift + dtr_ref[...])
        both = _mask_dot(cum01, g)
        g_cum = jnp.where(fwd_lane, both[:CHUNK], both[CHUNK:])
        gcol_scr[pl.ds(r0, CHUNK), :] = g_cum
        g8 = g_cum.T[:2 * HEADS]
        grow_scr[c] = jnp.concatenate([g8[:HEADS], g8[HEADS:]], axis=1)
        return carry

    lax.fori_loop(0, n_chunks, gates, 0, unroll=unroll)

    s_scr[...] = jnp.zeros(s_scr.shape, F32)
    o_ref[...] = jnp.zeros(o_ref.shape, F32)

    last = (CHUNK - 1, 0)
    head_lanes = [slice(h * HEAD_DIM, (h + 1) * HEAD_DIM) for h in range(HEADS)]
    pi = lax.broadcasted_iota(jnp.int32, (CHUNK, 2 * CHUNK), 0)
    pl_lane = lax.broadcasted_iota(jnp.int32, (CHUNK, 2 * CHUNK), 1)
    pj = pl_lane & (CHUNK - 1)
    first_half = pl_lane < CHUNK
    same_block = ((lax.broadcasted_iota(jnp.int32, (2 * CHUNK, 2 * CHUNK), 0) >> 6)
                  == (lax.broadcasted_iota(jnp.int32, (2 * CHUNK, 2 * CHUNK), 1) >> 6))
    incl = (pj <= pi, pj >= pi)
    strict = (pj < pi, pj > pi)
    zeros_k = jnp.zeros((CHUNK, HEAD_DIM), BF16)
    zeros_r = jnp.zeros((CHUNK, 2 * HEAD_DIM), F32)

    n_groups = n_chunks // group
    steps = [(d, t) for d in range(2) for t in range(group)]
    pairs = [(d, t, hp) for d, t in steps for hp in range(HEADS // 2)]

    def chunk_of(gi, d, t):
        i = gi * group + t
        return i if d == 0 else n_chunks - 1 - i

    def chunk_rows(c):
        return pl.ds(pl.multiple_of(c * CHUNK, CHUNK), CHUNK)

    def slot_index(d, t, h):
        return (d * HEADS + h) * group + t

    def prep_stages(gi, slot):
        rows = {(d, t): chunk_rows(chunk_of(gi, d, t)) for d, t in steps}
        q_c, k_c, kk, qk = {}, {}, [], []
        for d, t, hp in pairs:
            ha, hb = 2 * hp, 2 * hp + 1
            for h in (ha, hb):
                q_c[d, t, h] = q_ref[0, rows[d, t], head_lanes[h]]
                k_c[d, t, h] = k_ref[0, rows[d, t], head_lanes[h]]
            ka, kb = k_c[d, t, ha], k_c[d, t, hb]
            k_bd_t = jnp.concatenate([jnp.concatenate([ka, zeros_k], axis=1),
                                      jnp.concatenate([zeros_k, kb], axis=1)], axis=0)
            kq = jnp.concatenate([jnp.concatenate([ka, kb], axis=1),
                                  jnp.concatenate([q_c[d, t, ha], q_c[d, t, hb]], axis=1)], axis=0)
            both = _mm_nt(kq, k_bd_t)
            kk.append(both[:CHUNK])
            qk.append(both[CHUNK:])
        yield
        g_all = {dt: gcol_scr[rows[dt], :] for dt in steps}
        b_all = {dt: beta_scr[rows[dt], :] for dt in steps}
        g_rows = {(d, t): grow_scr[chunk_of(gi, d, t)] for d, t in steps}
        g_c, b_c, m, attn = {}, {}, [], []
        for a, (d, t, hp) in enumerate(pairs):
            for h in (2 * hp, 2 * hp + 1):
                col = _gate_col(d, h)
                g_c[d, t, h] = g_all[d, t][:, col:col + 1]
                b_c[d, t, h] = b_all[d, t][:, col:col + 1]
            g_cp = jnp.where(first_half, g_c[d, t, 2 * hp], g_c[d, t, 2 * hp + 1])
            b_cp = jnp.where(first_half, b_c[d, t, 2 * hp], b_c[d, t, 2 * hp + 1])
            row = d * (HEADS // 2) + hp
            g_rp = g_rows[d, t][row:row + 1, :]
            decay = jnp.exp(jnp.where(incl[d], g_cp - g_rp, -jnp.inf))
            m.append(jnp.where(strict[d], b_cp * kk[a] * decay, 0.0))
            attn.append(qk[a] * decay)
        x = []
        yield from _unit_tri_inverse_stages(m, pi, pj, same_block, x)
        for a, (d, t, hp) in enumerate(pairs):
            rhs = []
            for h in (2 * hp, 2 * hp + 1):
                k_f = k_c[d, t, h].astype(F32)
                v_f = v_ref[0, rows[d, t], head_lanes[h]].astype(F32)
                e_g = jnp.exp(g_c[d, t, h])
                g_last = g_c[d, t, h][last[d]:last[d] + 1, :]
                idx = slot_index(d, t, h)
                wq_scr[slot, idx, CHUNK:, :] = (q_c[d, t, h].astype(F32) * e_g).astype(BF16)
                ka_scr[slot, idx, :HEAD_DIM, :] = (k_f * jnp.exp(g_last - g_c[d, t, h])).T.astype(BF16)
                rhs.append(jnp.concatenate([v_f * b_c[d, t, h], k_f * (b_c[d, t, h] * e_g)], axis=1))
            rhs_bd = jnp.concatenate([jnp.concatenate([rhs[0], zeros_r], axis=1),
                                      jnp.concatenate([zeros_r, rhs[1]], axis=1)], axis=0)
            uw = _mm(x[a], rhs_bd)
            attn_b = attn[a].astype(BF16)
            for n, h in enumerate((2 * hp, 2 * hp + 1)):
                idx = slot_index(d, t, h)
                u_scr[slot, idx] = uw[:, 2 * n * HEAD_DIM:(2 * n + 1) * HEAD_DIM]
                wq_scr[slot, idx, :CHUNK, :] = (
                    uw[:, (2 * n + 1) * HEAD_DIM:(2 * n + 2) * HEAD_DIM].astype(BF16))
                ka_scr[slot, idx, HEAD_DIM:, :] = attn_b[:, n * CHUNK:(n + 1) * CHUNK]
        yield

    def scan_stages(gi, slot):
        for t in range(group):
            step = [(d, h) for d in range(2) for h in range(HEADS)]
            s = [s_scr[d * HEADS + h] for d, h in step]
            wqs = [_mm(wq_scr[slot, slot_index(d, t, h)], s[a]) for a, (d, h) in enumerate(step)]
            yield
            v_new = [u_scr[slot, slot_index(d, t, h)] - wqs[a][:CHUNK]
                     for a, (d, h) in enumerate(step)]
            kav = [_mm(ka_scr[slot, slot_index(d, t, h)], v_new[a]) for a, (d, h) in enumerate(step)]
            for a, (d, h) in enumerate(step):
                c = chunk_of(gi, d, t)
                col = _gate_col(d, h)
                edge = gcol_scr[pl.ds(pl.multiple_of(c * CHUNK + (last[d] // 8) * 8, 8), 8), :]
                g_last = edge[last[d] % 8:last[d] % 8 + 1, col:col + 1]
                s_scr[d * HEADS + h] = s[a] * jnp.exp(g_last) + kav[a][:HEAD_DIM]
                o_ref[0, chunk_rows(c), head_lanes[h]] += wqs[a][CHUNK:] + kav[a][HEAD_DIM:]
            yield

    for _ in prep_stages(0, 0):
        pass

    def body(gi, carry):
        slot = gi % 2
        for _ in itertools.zip_longest(prep_stages(gi + 1, 1 - slot), scan_stages(gi, slot)):
            pass
        return carry

    lax.fori_loop(0, n_groups - 1, body, 0)
    for _ in scan_stages(n_groups - 1, (n_groups - 1) % 2):
        pass


def _gdn(q, k, v, ba, alr, dtr):
    bsz, lp, _ = q.shape
    n_chunks = lp // CHUNK
    seq = lambda w: pl.BlockSpec((1, lp, w), lambda b: (b, 0, 0))
    small = lambda a: pl.BlockSpec(a.shape, lambda b: (0,) * a.ndim)
    group = _largest_divisor(n_chunks, MAX_PREP_GROUP)
    n_slot = 2 * HEADS * group
    return pl.pallas_call(
        functools.partial(_gdn_kernel, n_chunks=n_chunks, group=group,
                          unroll=_largest_divisor(n_chunks, MAX_GATE_UNROLL)),
        grid=(bsz,),
        in_specs=[seq(DN_WIDTH), seq(DN_WIDTH), seq(DN_WIDTH), seq(GATE_LANES),
                  small(alr), small(dtr)],
        out_specs=seq(DN_WIDTH),
        out_shape=jax.ShapeDtypeStruct((bsz, lp, DN_WIDTH), F32),
        scratch_shapes=[pltpu.VMEM((lp, GATE_LANES), F32),
                        pltpu.VMEM((lp, GATE_LANES), F32),
                        pltpu.VMEM((n_chunks, HEADS, 2 * CHUNK), F32),
                        pltpu.VMEM((2 * HEADS, HEAD_DIM, HEAD_DIM), F32),
                        pltpu.VMEM((2, n_slot, CHUNK, HEAD_DIM), F32),
                        pltpu.VMEM((2, n_slot, 2 * CHUNK, HEAD_DIM), BF16),
                        pltpu.VMEM((2, n_slot, HEAD_DIM + CHUNK, CHUNK), BF16)],
        compiler_params=pltpu.CompilerParams(
            dimension_semantics=("arbitrary",),
            vmem_limit_bytes=V7X_VMEM_LIMIT_BYTES),
        name="gdn",
    )(q, k, v, ba, alr, dtr)


def _outffn_kernel(h_ref, o_ref, z_ref, ysc_ref, dnw_ref, wo_ref, wsc_ref, n2_ref,
                   wg_ref, wu_ref, wd_ref, fw_ref, out_ref, og_scr, hn_scr, *, final):
    if final:
        h_ref, o_ref, z_ref, ysc_ref = (r.at[0] for r in (h_ref, o_ref, z_ref, ysc_ref))
    dnw = dnw_ref[...]
    for hd in range(HEADS):
        lanes = slice(hd * HEAD_DIM, (hd + 1) * HEAD_DIM)
        o = o_ref[:, lanes]
        o = o * lax.rsqrt(jnp.mean(o * o, axis=-1, keepdims=True) + RMS_EPS)
        og_scr[:, lanes] = (o * dnw[:, lanes] * _silu(z_ref[:, lanes].astype(F32))).astype(BF16)

    h1 = (h_ref[...]
          + jnp.dot(og_scr[...], wo_ref[...], preferred_element_type=F32)
          + jnp.dot(ysc_ref[...], wsc_ref[...], preferred_element_type=F32))
    out_ref[...] = h1
    ms = jnp.mean(h1 * h1, axis=-1, keepdims=True)
    hn_scr[...] = (h1 * lax.rsqrt(ms + RMS_EPS) * n2_ref[...]).astype(BF16)

    d_ff = wg_ref.shape[1]
    for f in range(d_ff // FF_BLOCK):
        cols = slice(f * FF_BLOCK, (f + 1) * FF_BLOCK)
        gate = jnp.dot(hn_scr[...], wg_ref[:, cols], preferred_element_type=F32)
        up = jnp.dot(hn_scr[...], wu_ref[:, cols], preferred_element_type=F32)
        act = (_silu(gate) * up).astype(BF16)
        out_ref[...] += jnp.dot(act, wd_ref[cols, :], preferred_element_type=F32)

    if final:
        h2 = out_ref[...]
        ms2 = jnp.mean(h2 * h2, axis=-1, keepdims=True)
        out_ref[...] = h2 * lax.rsqrt(ms2 + RMS_EPS) * fw_ref[...]


def _outffn(h, o, z, ysc, dnw, wo, wsc, n2, wg, wu, wd, fw, *, tm, skip_rows):
    bsz, lp, d = h.shape
    assert wg.shape[1] % FF_BLOCK == 0 and (lp - skip_rows) % tm == 0
    const = lambda a: pl.BlockSpec(a.shape, lambda b, j: (0,) * a.ndim, pipeline_mode=pl.Buffered(1))
    if skip_rows:
        row_in = lambda w: pl.BlockSpec((pl.Element(1), pl.Element(tm), pl.Element(w)),
                                        lambda b, j: (b, pl.multiple_of(skip_rows + j * tm, 8), 0))
    else:
        row_in = lambda w: pl.BlockSpec((None, tm, w), lambda b, j: (b, j, 0))
    return pl.pallas_call(
        functools.partial(_outffn_kernel, final=bool(skip_rows)),
        grid=(bsz, (lp - skip_rows) // tm),
        in_specs=[row_in(d), row_in(DN_WIDTH), row_in(DN_WIDTH), row_in(ysc.shape[-1]),
                  const(dnw), const(wo), const(wsc), const(n2),
                  const(wg), const(wu), const(wd), const(fw)],
        out_specs=pl.BlockSpec((None, tm, d), lambda b, j: (b, j, 0)),
        out_shape=jax.ShapeDtypeStruct((bsz, lp - skip_rows, d), F32),
        scratch_shapes=[pltpu.VMEM((tm, DN_WIDTH), BF16), pltpu.VMEM((tm, d), BF16)],
        compiler_params=pltpu.CompilerParams(
            dimension_semantics=("arbitrary", "arbitrary"),
            allow_input_fusion=[False] * 5 + [True, True, False, True, True, True, False],
            vmem_limit_bytes=V7X_VMEM_LIMIT_BYTES),
        name="outffn",
    )(h, o, z, ysc, dnw, wo, wsc, n2, wg, wu, wd, fw)


def kernel(x, meta_tokens, norm1_w, w_in, conv_qkv_w, a_log, dt_bias, dn_norm_w, sc_conv_w, w_out,
           norm2_w, w_gate_up, w_down, final_norm_w):
    bsz, seq, d = x.shape
    depth = w_in.shape[0]
    sc_width = sc_conv_w.shape[-1]
    d_ff = w_down.shape[1]
    assert conv_qkv_w.shape[-1] == 3 * DN_WIDTH and dn_norm_w.shape[-1] == HEAD_DIM
    assert w_in.shape[-1] == 4 * DN_WIDTH + 4 * HEADS + 3 * sc_width
    assert (N_PAD + N_META + seq) % CHUNK == 0
    lp = N_PAD + N_META + seq
    n_chunks = lp // CHUNK
    tm = _row_tile(lp)

    meta = jnp.broadcast_to(meta_tokens.astype(x.dtype)[None], (bsz, N_META, d))
    h = jnp.concatenate([jnp.zeros((bsz, N_PAD, d), x.dtype), meta, x], axis=1)

    gate_perm = [0] * (2 * HEADS)
    for dd in range(2):
        for hh in range(HEADS):
            gate_perm[_gate_col(dd, hh)] = dd * HEADS + hh
    gate_perm = jnp.array(gate_perm, jnp.int32)
    gate0 = 4 * DN_WIDTH
    sc0 = gate0 + 4 * HEADS
    for l in range(depth):
        wqkv = w_in[l, :, :3 * DN_WIDTH].astype(BF16)
        wz = w_in[l, :, 3 * DN_WIDTH:gate0].astype(BF16)
        wba = jnp.concatenate([w_in[l][:, gate0 + gate_perm], w_in[l][:, gate0 + 2 * HEADS + gate_perm]],
                              axis=1)
        wba = jnp.pad(wba, ((0, 0), (0, GATE_LANES - 4 * HEADS))).astype(BF16)
        wsc = w_in[l, :, sc0:].astype(BF16)
        q, k, v, z, ysc, ba = _inproj(h, norm1_w[l][None], wqkv, wz, wsc, wba,
                                      conv_qkv_w[l], sc_conv_w[l], tm=tm)

        pad_row = lambda t: jnp.pad(t.reshape(2 * HEADS)[gate_perm].astype(F32)[None],
                                    ((0, 0), (0, GATE_LANES - 2 * HEADS)))
        o = _gdn(q, k, v, ba, pad_row(a_log[l]), pad_row(dt_bias[l]))

        last = l == depth - 1
        h = _outffn(h, o, z, ysc,
                    jnp.tile(dn_norm_w[l], HEADS)[None].astype(F32),
                    w_out[l, :DN_WIDTH].astype(BF16), w_out[l, DN_WIDTH:].astype(BF16),
                    norm2_w[l][None],
                    w_gate_up[l, :, :d_ff].astype(BF16), w_gate_up[l, :, d_ff:].astype(BF16),
                    w_down[l].astype(BF16), final_norm_w[None],
                    tm=_final_tile(seq) if last else tm,
                    skip_rows=N_PAD + N_META if last else 0)

    return h
```
